```python
import jax, jax.numpy as jnp
from jax import lax
import numpy as np

D_MODEL = 2048
BATCH = 2
SEQ = 4096
DEPTH = 2
DEC_BATCH = 8
DEC_SEQ = 8
PAST_LEN = 16384
PAGE_SIZE = 128

SB_HEADS = 8
SB_HEAD_DIM = 128
D_SB = SB_HEADS * SB_HEAD_DIM
D_CONV = D_MODEL // 2
CONV_WIDTH = 31
D_FF = 5504
MEM_LEN = 256
MEM_HEADS = 4
MEM_HEAD_DIM = 128
D_MEM = MEM_HEADS * MEM_HEAD_DIM
Q_BLOCK = 128
EPS = 1e-6
QK_INIT_SCALE = 0.5
D_IN = 3 * D_SB + 2 * D_CONV + 2 * D_MODEL
SPLITS = [D_SB, 2 * D_SB, 3 * D_SB, 3 * D_SB + D_CONV, 3 * D_SB + 2 * D_CONV, 3 * D_SB + 2 * D_CONV + D_MODEL]

kernel_name = "stickbreak_conformer_hybrid_step"


def rms_norm(x, g):
    xf = x.astype(jnp.float32)
    y = xf * lax.rsqrt(jnp.mean(xf * xf, axis=-1, keepdims=True) + EPS)
    return (y * g.astype(jnp.float32)).astype(x.dtype)


def layer_norm(x, g, b):
    xf = x.astype(jnp.float32)
    mu = jnp.mean(xf, axis=-1, keepdims=True)
    xc = xf - mu
    y = xc * lax.rsqrt(jnp.mean(xc * xc, axis=-1, keepdims=True) + EPS)
    return (y * g.astype(jnp.float32) + b.astype(jnp.float32)).astype(x.dtype)


def swiglu(x, w_gate, w_up, w_down):
    return (jax.nn.silu(x @ w_gate) * (x @ w_up)) @ w_down


def stick_breaking_attention(q, k, v, sb_bias, q_offset):
    B, Lq, H, Dh = q.shape
    Lk = k.shape[1]
    blk = min(Q_BLOCK, Lq)
    n_blk = Lq // blk
    qb = q.reshape(B, n_blk, blk, H, Dh).transpose(1, 0, 2, 3, 4)
    key_pos = jnp.arange(Lk)
    scale = Dh ** -0.5
    bias = sb_bias.astype(jnp.float32)[None, :, None, None]

    def one_block(args):
        q_blk, blk_idx = args
        q_pos = q_offset + blk_idx * blk + jnp.arange(blk)
        z = jnp.einsum("bqhd,bkhd->bhqk", q_blk, k).astype(jnp.float32) * scale + bias
        mask = key_pos[None, :] < q_pos[:, None]
        log_beta = jax.nn.log_sigmoid(z)
        log_keep = jnp.where(mask, jax.nn.log_sigmoid(-z), 0.0)
        tail = lax.cumsum(log_keep, axis=3, reverse=True) - log_keep
        w = jnp.where(mask, jnp.exp(log_beta + tail), 0.0)
        return jnp.einsum("bhqk,bkhd->bqhd", w.astype(v.dtype), v)

    out = lax.map(one_block, (qb, jnp.arange(n_blk)))
    return out.transpose(1, 0, 2, 3, 4).reshape(B, Lq, H, Dh)


def conformer_conv(c_val, c_gate, conv_buf, w_dw, b_dw, ln_g, ln_b, w_o):
    g = c_val * jax.nn.sigmoid(c_gate)
    ext = jnp.concatenate([conv_buf.astype(g.dtype), g], axis=1)
    y = lax.conv_general_dilated(ext, w_dw[:, None, :].astype(g.dtype), window_strides=(1,), padding="VALID",
                                 dimension_numbers=("NWC", "WIO", "NWC"), feature_group_count=D_CONV) + b_dw
    new_buf = ext[:, ext.shape[1] - (CONV_WIDTH - 1):]
    y = jax.nn.silu(layer_norm(y, ln_g, ln_b))
    return y @ w_o, new_buf


def memory_kv(mem, g, w_ck, w_cv):
    B, M, _ = mem.shape
    m = rms_norm(mem, g)
    return (m @ w_ck).reshape(B, M, MEM_HEADS, MEM_HEAD_DIM), (m @ w_cv).reshape(B, M, MEM_HEADS, MEM_HEAD_DIM)


def cross_attend(u, w_cq, mem_k, mem_v, w_co):
    B, L, _ = u.shape
    q = (u @ w_cq).reshape(B, L, MEM_HEADS, MEM_HEAD_DIM)
    s = jnp.einsum("bqhd,bmhd->bhqm", q, mem_k).astype(jnp.float32) * (MEM_HEAD_DIM ** -0.5)
    p = jax.nn.softmax(s, axis=-1).astype(mem_v.dtype)
    o = jnp.einsum("bhqm,bmhd->bqhd", p, mem_v).reshape(B, L, D_MEM)
    return o @ w_co


def hybrid_layer(x, mem_k, mem_v, k_past, v_past, conv_buf, q_offset,
                 norm_ffa, ffa_w_gate, ffa_w_up, ffa_w_down,
                 norm_mix, w_in, sb_bias, conv_w_dw, conv_b_dw, conv_ln_g, conv_ln_b, w_conv_o, w_sb_o, w_mix_o,
                 norm_cross, w_cq, w_co,
                 norm_ffb, ffb_w_gate, ffb_w_up, ffb_w_down):
    B, L, _ = x.shape
    x = x + 0.5 * swiglu(rms_norm(x, norm_ffa), ffa_w_gate, ffa_w_up, ffa_w_down)
    u = rms_norm(x, norm_mix)
    q, k, v, c_val, c_gate, g_sb, g_conv = jnp.split(u @ w_in, SPLITS, axis=-1)
    q = q.reshape(B, L, SB_HEADS, SB_HEAD_DIM)
    k = k.reshape(B, L, SB_HEADS, SB_HEAD_DIM)
    v = v.reshape(B, L, SB_HEADS, SB_HEAD_DIM)
    if k_past is None:
        k_all, v_all = k, v
    else:
        k_all = jnp.concatenate([k_past.astype(k.dtype), k], axis=1)
        v_all = jnp.concatenate([v_past.astype(v.dtype), v], axis=1)
    o_sb = stick_breaking_attention(q, k_all, v_all, sb_bias, q_offset).reshape(B, L, D_SB) @ w_sb_o
    o_conv, new_buf = conformer_conv(c_val, c_gate, conv_buf, conv_w_dw, conv_b_dw, conv_ln_g, conv_ln_b, w_conv_o)
    mixed = jax.nn.sigmoid(g_sb) * o_sb + jax.nn.sigmoid(g_conv) * o_conv
    x = x + mixed @ w_mix_o
    x = x + cross_attend(rms_norm(x, norm_cross), w_cq, mem_k, mem_v, w_co)
    x = x + 0.5 * swiglu(rms_norm(x, norm_ffb), ffb_w_gate, ffb_w_up, ffb_w_down)
    return x, k, v, new_buf


def setup_inputs(seed: int = 0) -> dict:
    key = jax.random.key(seed)
    ks = jax.random.split(key, 40)
    n_pages = PAST_LEN // PAGE_SIZE
    n_used = DEC_BATCH * n_pages
    n_pool = n_used + n_used // 4
    nrm = lambda k, shape, s: jax.random.normal(k, shape, jnp.float32) * s
    gain = lambda k: 1.0 + 0.05 * jax.random.normal(k, (DEPTH, D_MODEL), jnp.float32)
    page_table = jax.random.permutation(ks[0], n_pool)[:n_used].reshape(DEC_BATCH, n_pages).astype(jnp.int32)
    col_scale = jnp.concatenate([jnp.full((2 * D_SB,), QK_INIT_SCALE, jnp.float32),
                                 jnp.ones((D_IN - 2 * D_SB,), jnp.float32)])
    w_in = nrm(ks[14], (DEPTH, D_MODEL, D_IN), D_MODEL ** -0.5) * col_scale
    sb_bias = -jnp.linspace(3.0, 9.0, SB_HEADS, dtype=jnp.float32)[None, :] + nrm(ks[33], (DEPTH, SB_HEADS), 0.1)
    return {
        "x_prompt": nrm(ks[1], (BATCH, SEQ, D_MODEL), 1.0),
        "x_sample": nrm(ks[2], (DEC_BATCH, DEC_SEQ, D_MODEL), 1.0),
        "mem_prompt": nrm(ks[3], (BATCH, MEM_LEN, D_MODEL), 1.0),
        "cache_k": nrm(ks[4], (DEPTH, n_pool, PAGE_SIZE, SB_HEADS, SB_HEAD_DIM), QK_INIT_SCALE),
        "cache_v": nrm(ks[5], (DEPTH, n_pool, PAGE_SIZE, SB_HEADS, SB_HEAD_DIM), 1.0),
        "state_conv": nrm(ks[6], (DEPTH, DEC_BATCH, CONV_WIDTH - 1, D_CONV), 0.5),
        "cache_mem_k": nrm(ks[7], (DEPTH, DEC_BATCH, MEM_LEN, MEM_HEADS, MEM_HEAD_DIM), 1.0),
        "cache_mem_v": nrm(ks[8], (DEPTH, DEC_BATCH, MEM_LEN, MEM_HEADS, MEM_HEAD_DIM), 1.0),
        "page_table": page_table,
        "norm_ffa": gain(ks[9]),
        "ffa_w_gate": nrm(ks[10], (DEPTH, D_MODEL, D_FF), D_MODEL ** -0.5),
        "ffa_w_up": nrm(ks[11], (DEPTH, D_MODEL, D_FF), D_MODEL ** -0.5),
        "ffa_w_down": nrm(ks[12], (DEPTH, D_FF, D_MODEL), D_FF ** -0.5),
        "norm_mix": gain(ks[13]),
        "w_in": w_in,
        "sb_bias": sb_bias,
        "conv_w_dw": nrm(ks[15], (DEPTH, CONV_WIDTH, D_CONV), CONV_WIDTH ** -0.5),
        "conv_b_dw": nrm(ks[16], (DEPTH, D_CONV), 0.01),
        "conv_ln_g": 1.0 + 0.05 * jax.random.normal(ks[17], (DEPTH, D_CONV), jnp.float32),
        "conv_ln_b": nrm(ks[18], (DEPTH, D_CONV), 0.01),
        "w_conv_o": nrm(ks[19], (DEPTH, D_CONV, D_MODEL), D_CONV ** -0.5),
        "w_sb_o": nrm(ks[20], (DEPTH, D_SB, D_MODEL), D_SB ** -0.5),
        "w_mix_o": nrm(ks[21], (DEPTH, D_MODEL, D_MODEL), D_MODEL ** -0.5),
        "norm_cross": gain(ks[22]),
        "norm_mem": gain(ks[23]),
        "w_cq": nrm(ks[24], (DEPTH, D_MODEL, D_MEM), D_MODEL ** -0.5),
        "w_ck": nrm(ks[25], (DEPTH, D_MODEL, D_MEM), D_MODEL ** -0.5),
        "w_cv": nrm(ks[26], (DEPTH, D_MODEL, D_MEM), D_MODEL ** -0.5),
        "w_co": nrm(ks[27], (DEPTH, D_MEM, D_MODEL), D_MEM ** -0.5),
        "norm_ffb": gain(ks[28]),
        "ffb_w_gate": nrm(ks[29], (DEPTH, D_MODEL, D_FF), D_MODEL ** -0.5),
        "ffb_w_up": nrm(ks[30], (DEPTH, D_MODEL, D_FF), D_MODEL ** -0.5),
        "ffb_w_down": nrm(ks[31], (DEPTH, D_FF, D_MODEL), D_FF ** -0.5),
        "norm_final": 1.0 + 0.05 * jax.random.normal(ks[32], (D_MODEL,), jnp.float32),
    }


def reference(x_prompt, x_sample, mem_prompt, cache_k, cache_v, state_conv, cache_mem_k, cache_mem_v, page_table,
              norm_ffa, ffa_w_gate, ffa_w_up, ffa_w_down,
              norm_mix, w_in, sb_bias, conv_w_dw, conv_b_dw, conv_ln_g, conv_ln_b, w_conv_o, w_sb_o, w_mix_o,
              norm_cross, norm_mem, w_cq, w_ck, w_cv, w_co,
              norm_ffb, ffb_w_gate, ffb_w_up, ffb_w_down, norm_final):
    dec_b, n_pages = page_table.shape
    past_len = n_pages * PAGE_SIZE
    yp, ys = x_prompt, x_sample
    kp_l, vp_l, cp_l, mk_l, mv_l, ks_l, vs_l, cs_l = [], [], [], [], [], [], [], []
    zero_buf = jnp.zeros((x_prompt.shape[0], CONV_WIDTH - 1, D_CONV), x_prompt.dtype)
    for l in range(DEPTH):
        w = (norm_ffa[l], ffa_w_gate[l], ffa_w_up[l], ffa_w_down[l],
             norm_mix[l], w_in[l], sb_bias[l], conv_w_dw[l], conv_b_dw[l], conv_ln_g[l], conv_ln_b[l], w_conv_o[l],
             w_sb_o[l], w_mix_o[l],
             norm_cross[l], w_cq[l], w_co[l],
             norm_ffb[l], ffb_w_gate[l], ffb_w_up[l], ffb_w_down[l])
        mk, mv = memory_kv(mem_prompt, norm_mem[l], w_ck[l], w_cv[l])
        yp, kp, vp, cp = hybrid_layer(yp, mk, mv, None, None, zero_buf, 0, *w)
        k_past = cache_k[l][page_table].reshape(dec_b, past_len, SB_HEADS, SB_HEAD_DIM)
        v_past = cache_v[l][page_table].reshape(dec_b, past_len, SB_HEADS, SB_HEAD_DIM)
        ys, kn, vn, cn = hybrid_layer(ys, cache_mem_k[l], cache_mem_v[l], k_past, v_past, state_conv[l], past_len, *w)
        kp_l.append(kp); vp_l.append(vp); cp_l.append(cp); mk_l.append(mk); mv_l.append(mv)
        ks_l.append(kn); vs_l.append(vn); cs_l.append(cn)
    y_prompt = rms_norm(yp, norm_final)
    y_sample = rms_norm(ys, norm_final)
    return (y_prompt, y_sample, jnp.stack(kp_l), jnp.stack(vp_l), jnp.stack(cp_l), jnp.stack(mk_l), jnp.stack(mv_l),
            jnp.stack(ks_l), jnp.stack(vs_l), jnp.stack(cs_l))
```

```python
import functools

import jax
import jax.numpy as jnp
from jax import lax
from jax.experimental import pallas as pl
from jax.experimental.pallas import tpu as pltpu

EPS = 1e-6
PAGE = 128
HEAD_DIM = 128
CONV_W = 31
HALO = 32
VMEM_LIMIT = 56 * 1024 * 1024

BF16 = jnp.bfloat16
F32 = jnp.float32


def _cparams(n_axes):
    return pltpu.CompilerParams(dimension_semantics=("arbitrary",) * n_axes, vmem_limit_bytes=VMEM_LIMIT)


def _rms(x, g):
    return x * lax.rsqrt(jnp.mean(x * x, axis=-1, keepdims=True) + EPS) * g


def _dot(a, b):
    return jnp.dot(a, b, preferred_element_type=F32)


def _dot_nt(a, b):
    return lax.dot_general(a, b, (((1,), (1,)), ((), ())), preferred_element_type=F32)


def _sigmoid(x):
    return 1.0 / (1.0 + jnp.exp(-x))


def _ffn_kernel(x_ref, g_ref, wg_ref, wu_ref, wd_ref, gf_ref, o_ref, u_ref, *, nf, tf, tail, final_norm):
    f = pl.program_id(1)

    @pl.when(f == 0)
    def _():
        u_ref[...] = _rms(x_ref[...], g_ref[...]).astype(BF16)
        o_ref[...] = jnp.zeros_like(o_ref)

    def step(width):
        u = u_ref[...]
        gate = _dot(u, wg_ref[:, :width].astype(BF16))
        up = _dot(u, wu_ref[:, :width].astype(BF16))
        h = (gate * _sigmoid(gate) * up).astype(BF16)
        o_ref[...] += _dot(h, wd_ref[:width, :].astype(BF16))

    if tail == tf:
        step(tf)
    else:
        @pl.when(f < nf - 1)
        def _():
            step(tf)

        @pl.when(f == nf - 1)
        def _():
            step(tail)

    @pl.when(f == nf - 1)
    def _():
        y = x_ref[...] + 0.5 * o_ref[...]
        if final_norm:
            y = _rms(y, gf_ref[...])
        o_ref[...] = y


def _ffn(x, g, wg, wu, wd, gfin, *, tm, tf, final_norm):
    m, d = x.shape
    dff = wg.shape[1]
    nf = pl.cdiv(dff, tf)
    tail = dff - (nf - 1) * tf
    kern = functools.partial(_ffn_kernel, nf=nf, tf=tf, tail=tail, final_norm=final_norm)
    return pl.pallas_call(
        kern,
        out_shape=jax.ShapeDtypeStruct((m, d), F32),
        grid=(m // tm, nf),
        in_specs=[
            pl.BlockSpec((tm, d), lambda i, f: (i, 0)),
            pl.BlockSpec((1, d), lambda i, f: (0, 0)),
            pl.BlockSpec((d, tf), lambda i, f: (0, f)),
            pl.BlockSpec((d, tf), lambda i, f: (0, f)),
            pl.BlockSpec((tf, d), lambda i, f: (f, 0)),
            pl.BlockSpec((1, d), lambda i, f: (0, 0)),
        ],
        out_specs=pl.BlockSpec((tm, d), lambda i, f: (i, 0)),
        scratch_shapes=[pltpu.VMEM((tm, d), BF16)],
        compiler_params=_cparams(2),
        name="ffn",
    )(x, g.reshape(1, d), wg, wu, wd, gfin.reshape(1, d))


def _norm_matmul_kernel(x_ref, g_ref, w_ref, o_ref, u_ref):
    @pl.when(pl.program_id(1) == 0)
    def _():
        u_ref[...] = _rms(x_ref[...], g_ref[...]).astype(BF16)

    o_ref[...] = _dot(u_ref[...], w_ref[...].astype(BF16))


def _norm_matmul(x, g, w, *, tm, tn, name):
    m, d = x.shape
    n = w.shape[1]
    return pl.pallas_call(
        _norm_matmul_kernel,
        out_shape=jax.ShapeDtypeStruct((m, n), F32),
        grid=(m // tm, n // tn),
        in_specs=[
            pl.BlockSpec((tm, d), lambda i, j: (i, 0)),
            pl.BlockSpec((1, d), lambda i, j: (0, 0)),
            pl.BlockSpec((d, tn), lambda i, j: (0, j)),
        ],
        out_specs=pl.BlockSpec((tm, tn), lambda i, j: (i, j)),
        scratch_shapes=[pltpu.VMEM((tm, d), BF16)],
        compiler_params=_cparams(2),
        name=name,
    )(x, g.reshape(1, d), w)


def _sb_block(z, carry, tri, mask):
    keys = z.shape[1]
    soft = jnp.log1p(jnp.exp(-jnp.abs(z)))
    log_beta = jnp.minimum(z, 0.0) - soft
    log_keep = -jnp.maximum(z, 0.0) - soft
    if mask is not None:
        log_keep = jnp.where(mask, log_keep, 0.0)
    hi = log_keep.astype(BF16)
    lo = (log_keep - hi.astype(F32)).astype(BF16)
    sums = _dot(hi, tri) + _dot(lo, tri)
    tail = sums[:, :keys]
    total = sums[:, keys:]
    carry_b = carry if keys == HEAD_DIM else jnp.concatenate([carry] * (keys // HEAD_DIM), axis=1)
    w = jnp.exp(log_beta + tail + carry_b)
    if mask is not None:
        w = jnp.where(mask, w, 0.0)
    return w.astype(BF16), carry + total


def _tri(keys):
    j = lax.broadcasted_iota(jnp.int32, (keys, keys), 0)
    s = lax.broadcasted_iota(jnp.int32, (keys, keys), 1)
    return jnp.concatenate([(j > s).astype(BF16), jnp.ones((keys, HEAD_DIM), BF16)], axis=1)


def _sbp_kernel(bias_ref, q_ref, k_ref, v_ref, tri_ref, o_ref, kb_ref, vb_ref, *, tq, scale):
    h = pl.program_id(1)
    i = pl.program_id(2)

    @pl.when(i == 0)
    def _():
        kb_ref[...] = k_ref[...].astype(BF16)
        vb_ref[...] = v_ref[...].astype(BF16)

    bias = bias_ref[h]
    q = q_ref[...].astype(BF16)
    tri = tri_ref[...]

    def block(j, carry, acc, mask):
        start = pl.multiple_of(j * tq, tq)
        z = _dot_nt(q, kb_ref[pl.ds(start, tq), :]) * scale + bias
        w, carry = _sb_block(z, carry, tri, mask)
        return carry, acc + _dot(w, vb_ref[pl.ds(start, tq), :])

    row = lax.broadcasted_iota(jnp.int32, (tq, tq), 0)
    col = lax.broadcasted_iota(jnp.int32, (tq, tq), 1)
    zeros = jnp.zeros((tq, HEAD_DIM), F32)
    carry, acc = block(i, zeros, zeros, col < row)

    def body(n, st):
        return block(i - 1 - n, st[0], st[1], None)

    _, acc = lax.fori_loop(0, i, body, (carry, acc))
    o_ref[...] = acc.astype(o_ref.dtype)


def _sb_prompt(proj, sb_bias, *, batch, seq, heads, tq):
    nq = seq // tq
    kern = functools.partial(_sbp_kernel, tq=tq, scale=HEAD_DIM ** -0.5)
    grid_spec = pltpu.PrefetchScalarGridSpec(
        num_scalar_prefetch=1,
        grid=(batch, heads, nq),
        in_specs=[
            pl.BlockSpec((tq, HEAD_DIM), lambda b, h, i, bias: (b * nq + i, h)),
            pl.BlockSpec((seq, HEAD_DIM), lambda b, h, i, bias: (b, heads + h)),
            pl.BlockSpec((seq, HEAD_DIM), lambda b, h, i, bias: (b, 2 * heads + h)),
            pl.BlockSpec((tq, tq + HEAD_DIM), lambda b, h, i, bias: (0, 0)),
        ],
        out_specs=pl.BlockSpec((tq, HEAD_DIM), lambda b, h, i, bias: (b * nq + i, h)),
        scratch_shapes=[pltpu.VMEM((seq, HEAD_DIM), BF16), pltpu.VMEM((seq, HEAD_DIM), BF16)],
    )
    return pl.pallas_call(
        kern,
        out_shape=jax.ShapeDtypeStruct((proj.shape[0], heads * HEAD_DIM), F32),
        grid_spec=grid_spec,
        compiler_params=_cparams(3),
        name="sb_prompt",
    )(sb_bias, proj, proj, proj, _tri(tq))


def _sbs_kernel(pt_ref, q_ref, kn_ref, vn_ref, kp_ref, vp_ref, bias_ref, tri_ref, o_ref,
                wq_ref, kpad_ref, vpad_ref, carry_ref, acc_ref, *, heads, lq, n_steps, scale):
    j = pl.program_id(1)
    rows = heads * lq
    width = heads * HEAD_DIM

    def block(k, v, mask):
        z = _dot_nt(wq_ref[...], k) * scale + bias_ref[...]
        w, carry = _sb_block(z, carry_ref[...], tri_ref[...], mask)
        carry_ref[...] = carry
        acc_ref[...] += _dot(w, v)

    @pl.when(j == 0)
    def _():
        q = q_ref[...]
        lane_head = lax.broadcasted_iota(jnp.int32, (lq, width), 1) >> (HEAD_DIM.bit_length() - 1)
        for h in range(heads):
            wq_ref[h * lq:(h + 1) * lq, :] = jnp.where(lane_head == h, q, 0.0).astype(BF16)
        carry_ref[...] = jnp.zeros_like(carry_ref)
        acc_ref[...] = jnp.zeros_like(acc_ref)
        kpad_ref[...] = jnp.zeros_like(kpad_ref)
        vpad_ref[...] = jnp.zeros_like(vpad_ref)
        kpad_ref[0:lq, :] = kn_ref[...]
        vpad_ref[0:lq, :] = vn_ref[...]
        key = lax.broadcasted_iota(jnp.int32, (rows, PAGE), 1)
        t = lax.broadcasted_iota(jnp.int32, (rows, PAGE), 0) & (lq - 1)
        block(kpad_ref[...].astype(BF16), vpad_ref[...].astype(BF16), key < t)

    @pl.when(j > 0)
    def _():
        block(kp_ref[...].astype(BF16), vp_ref[...].astype(BF16), None)

    @pl.when(j == n_steps - 1)
    def _():
        for h in range(heads):
            o_ref[:, h * HEAD_DIM:(h + 1) * HEAD_DIM] = acc_ref[
                h * lq:(h + 1) * lq, h * HEAD_DIM:(h + 1) * HEAD_DIM].astype(o_ref.dtype)


def _sb_sample(attn, proj, cache_k, cache_v, page_table, sb_bias, *, layer, row0, heads, lq):
    dec_b, n_pages = page_table.shape
    width = heads * HEAD_DIM
    rows = heads * lq
    n_steps = n_pages + 1
    rb0 = row0 // lq
    kern = functools.partial(_sbs_kernel, heads=heads, lq=lq, n_steps=n_steps, scale=HEAD_DIM ** -0.5)

    def page(b, j, pt):
        return (layer, pt[b, jnp.minimum(n_pages - j, n_pages - 1)], 0, 0)

    grid_spec = pltpu.PrefetchScalarGridSpec(
        num_scalar_prefetch=1,
        grid=(dec_b, n_steps),
        in_specs=[
            pl.BlockSpec((lq, width), lambda b, j, pt: (rb0 + b, 0)),
            pl.BlockSpec((lq, width), lambda b, j, pt: (rb0 + b, 1)),
            pl.BlockSpec((lq, width), lambda b, j, pt: (rb0 + b, 2)),
            pl.BlockSpec((None, None, PAGE, width), page),
            pl.BlockSpec((None, None, PAGE, width), page),
            pl.BlockSpec((rows, PAGE), lambda b, j, pt: (0, 0)),
            pl.BlockSpec((PAGE, PAGE + HEAD_DIM), lambda b, j, pt: (0, 0)),
            pl.BlockSpec(memory_space=pl.ANY),
        ],
        out_specs=pl.BlockSpec((lq, width), lambda b, j, pt: (rb0 + b, 0)),
        scratch_shapes=[
            pltpu.VMEM((rows, width), BF16),
            pltpu.VMEM((PAGE, width), F32),
            pltpu.VMEM((PAGE, width), F32),
            pltpu.VMEM((rows, HEAD_DIM), F32),
            pltpu.VMEM((rows, width), F32),
        ],
    )
    bias_rows = jnp.broadcast_to(jnp.repeat(sb_bias, lq)[:, None], (rows, PAGE))

    def kern_aliased(pt_ref, q_ref, kn_ref, vn_ref, kp_ref, vp_ref, bias_ref, tri_ref, attn_in_ref, o_ref, *scratch):
        del attn_in_ref
        kern(pt_ref, q_ref, kn_ref, vn_ref, kp_ref, vp_ref, bias_ref, tri_ref, o_ref, *scratch)

    return pl.pallas_call(
        kern_aliased,
        out_shape=jax.ShapeDtypeStruct(attn.shape, attn.dtype),
        grid_spec=grid_spec,
        input_output_aliases={8: 0},
        compiler_params=_cparams(2),
        name="sb_sample",
    )(page_table, proj, proj, proj, cache_k, cache_v, bias_rows, _tri(PAGE), attn)


def _conv_kernel(cv_ref, cg_ref, init_ref, w_ref, b_ref, lg_ref, lb_ref, *rest, tq, nt, rc, cc):
    y_ref, buf_ref, ext_ref, acc_ref = rest[-4:]
    i = pl.program_id(1)
    ch = cv_ref.shape[1]

    @pl.when(i == 0)
    def _():
        ext_ref[0:HALO, :] = init_ref[...]

    ext_ref[HALO:HALO + tq, :] = cv_ref[...] * _sigmoid(cg_ref[...])
    first = HALO - (CONV_W - 1)
    for r in range(0, tq, rc):
        for c in range(0, ch, cc):
            acc = jnp.broadcast_to(b_ref[:, c:c + cc], (rc, cc))
            for j in range(CONV_W):
                acc = acc + ext_ref[r + first + j:r + first + j + rc, c:c + cc] * w_ref[j:j + 1, c:c + cc]
            acc_ref[r:r + rc, c:c + cc] = acc
    y = acc_ref[...]
    mu = jnp.mean(y, axis=-1, keepdims=True)
    yc = y - mu
    y = yc * lax.rsqrt(jnp.mean(yc * yc, axis=-1, keepdims=True) + EPS) * lg_ref[...] + lb_ref[...]
    y_ref[...] = (y * _sigmoid(y)).astype(y_ref.dtype)

    @pl.when(i == nt - 1)
    def _():
        buf_ref[...] = ext_ref[tq:tq + HALO, :]

    if nt > 1:
        @pl.when(i < nt - 1)
        def _():
            ext_ref[0:HALO, :] = ext_ref[tq:tq + HALO, :]


def _conv(yact, proj, init, w_dw, b_dw, ln_g, ln_b, *, batch, seq, row0, tq, val_blk):
    ch = w_dw.shape[1]
    aliased = yact is not None
    nt = seq // tq
    rb0 = row0 // tq
    rc = min(tq, 64)
    kern = functools.partial(_conv_kernel, tq=tq, nt=nt, rc=rc, cc=256)
    row_map = lambda b, i: (rb0 + b * nt + i, 0)
    vec = pl.BlockSpec((1, ch), lambda b, i: (0, 0))
    in_specs = [
        pl.BlockSpec((tq, ch), lambda b, i: (rb0 + b * nt + i, val_blk)),
        pl.BlockSpec((tq, ch), lambda b, i: (rb0 + b * nt + i, val_blk + 1)),
        pl.BlockSpec((None, HALO, ch), lambda b, i: (b, 0, 0)),
        pl.BlockSpec((CONV_W, ch), lambda b, i: (0, 0)),
        vec, vec, vec,
    ]
    args = [proj, proj, init, w_dw, b_dw.reshape(1, ch), ln_g.reshape(1, ch), ln_b.reshape(1, ch)]
    if aliased:
        in_specs.append(pl.BlockSpec(memory_space=pl.ANY))
        args.append(yact)
    return pl.pallas_call(
        kern,
        out_shape=(jax.ShapeDtypeStruct((proj.shape[0], ch), F32), jax.ShapeDtypeStruct((batch, HALO, ch), F32)),
        grid=(batch, nt),
        in_specs=in_specs,
        out_specs=(pl.BlockSpec((tq, ch), row_map), pl.BlockSpec((None, HALO, ch), lambda b, i: (b, 0, 0))),
        scratch_shapes=[pltpu.VMEM((HALO + tq, ch), F32), pltpu.VMEM((tq, ch), F32)],
        input_output_aliases={len(args) - 1: 0} if aliased else {},
        compiler_params=_cparams(2),
        name="conv_sample" if aliased else "conv_prompt",
    )(*args)


def _mix_kernel(x_ref, a_ref, y_ref, gs_ref, gc_ref, wsb_ref, wcv_ref, wmx_ref, o_ref, *, nn):
    n = pl.program_id(1)

    @pl.when(n == 0)
    def _():
        o_ref[...] = jnp.zeros_like(o_ref)

    o_sb = _dot(a_ref[...].astype(BF16), wsb_ref[...].astype(BF16))
    o_cv = _dot(y_ref[...].astype(BF16), wcv_ref[...].astype(BF16))
    mixed = _sigmoid(gs_ref[...]) * o_sb + _sigmoid(gc_ref[...]) * o_cv
    o_ref[...] += _dot(mixed.astype(BF16), wmx_ref[...].astype(BF16))

    @pl.when(n == nn - 1)
    def _():
        o_ref[...] = x_ref[...] + o_ref[...]


def _mix(x, attn, yact, proj, w_sb_o, w_conv_o, w_mix_o, *, tm, tn, gsb_col, gconv_col):
    m, d = x.shape
    d_sb = attn.shape[1]
    d_conv = yact.shape[1]
    nn = d // tn
    kern = functools.partial(_mix_kernel, nn=nn)
    return pl.pallas_call(
        kern,
        out_shape=jax.ShapeDtypeStruct((m, d), F32),
        grid=(m // tm, nn),
        in_specs=[
            pl.BlockSpec((tm, d), lambda i, n: (i, 0)),
            pl.BlockSpec((tm, d_sb), lambda i, n: (i, 0)),
            pl.BlockSpec((tm, d_conv), lambda i, n: (i, 0)),
            pl.BlockSpec((tm, tn), lambda i, n: (i, gsb_col // tn + n)),
            pl.BlockSpec((tm, tn), lambda i, n: (i, gconv_col // tn + n)),
            pl.BlockSpec((d_sb, tn), lambda i, n: (0, n)),
            pl.BlockSpec((d_conv, tn), lambda i, n: (0, n)),
            pl.BlockSpec((tn, d), lambda i, n: (n, 0)),
        ],
        out_specs=pl.BlockSpec((tm, d), lambda i, n: (i, 0)),
        compiler_params=_cparams(2),
        name="mix",
    )(x, attn, yact, proj, proj, w_sb_o, w_conv_o, w_mix_o)


def _cross_kernel(x_ref, g_ref, wq_ref, mk_ref, mv_ref, wo_ref, o_ref, *, heads, scale):
    x = x_ref[...]
    u = _rms(x, g_ref[...]).astype(BF16)
    q = _dot(u, wq_ref[...].astype(BF16))
    outs = []
    for h in range(heads):
        sl = slice(h * HEAD_DIM, (h + 1) * HEAD_DIM)
        s = _dot_nt(q[:, sl].astype(BF16), mk_ref[:, sl].astype(BF16)) * scale
        p = jnp.exp(s - jnp.max(s, axis=-1, keepdims=True))
        o = _dot(p.astype(BF16), mv_ref[:, sl].astype(BF16)) / jnp.sum(p, axis=-1, keepdims=True)
        outs.append(o.astype(BF16))
    o_ref[...] = x + _dot(jnp.concatenate(outs, axis=1), wo_ref[...].astype(BF16))


def _cross(x, g, w_cq, mem_k, mem_v, w_co, *, batch, seq, row0, tm, heads):
    m, d = x.shape
    d_mem = w_cq.shape[1]
    mem_len = mem_k.shape[0] // batch
    nt = seq // tm
    rb0 = row0 // tm
    kern = functools.partial(_cross_kernel, heads=heads, scale=HEAD_DIM ** -0.5)
    return pl.pallas_call(
        kern,
        out_shape=jax.ShapeDtypeStruct((m, d), F32),
        grid=(batch, nt),
        in_specs=[
            pl.BlockSpec((tm, d), lambda b, i: (rb0 + b * nt + i, 0)),
            pl.BlockSpec((1, d), lambda b, i: (0, 0)),
            pl.BlockSpec((d, d_mem), lambda b, i: (0, 0)),
            pl.BlockSpec((mem_len, d_mem), lambda b, i: (b, 0)),
            pl.BlockSpec((mem_len, d_mem), lambda b, i: (b, 0)),
            pl.BlockSpec((d_mem, d), lambda b, i: (0, 0)),
        ],
        out_specs=pl.BlockSpec((tm, d), lambda b, i: (rb0 + b * nt + i, 0)),
        input_output_aliases={0: 0},
        compiler_params=_cparams(2),
        name="cross",
    )(x, g.reshape(1, d), w_cq, mem_k, mem_v, w_co)


def _largest_tile(n, cap, mult):
    best = None
    for t in range(mult, cap + 1, mult):
        if n % t == 0:
            best = t
    assert best is not None, (n, cap, mult)
    return best


def kernel(x_prompt, x_sample, mem_prompt, cache_k, cache_v, state_conv, cache_mem_k, cache_mem_v, page_table, norm_ffa, ffa_w_gate, ffa_w_up, ffa_w_down, norm_mix, w_in, sb_bias, conv_w_dw, conv_b_dw, conv_ln_g, conv_ln_b, w_conv_o, w_sb_o, w_mix_o, norm_cross, norm_mem, w_cq, w_ck, w_cv, w_co, norm_ffb, ffb_w_gate, ffb_w_up, ffb_w_down, norm_final):
    batch, seq, d = x_prompt.shape
    dec_b, dec_seq, _ = x_sample.shape
    depth = w_in.shape[0]
    heads = cache_k.shape[3]
    d_sb = heads * HEAD_DIM
    d_conv = conv_w_dw.shape[2]
    mem_len = mem_prompt.shape[1]
    mem_heads = cache_mem_k.shape[3]
    d_mem = mem_heads * HEAD_DIM
    n_prompt = batch * seq
    n_tok = n_prompt + dec_b * dec_seq
    val_col, gsb_col, gconv_col = 3 * d_sb, 3 * d_sb + 2 * d_conv, 3 * d_sb + 2 * d_conv + d
    assert d_conv == d_sb and val_col % d_conv == 0

    tm = _largest_tile(n_tok, 768, 16)
    tq_attn = min(seq, 256)
    tq_conv = min(seq, 256)
    tm_cross = min(seq, 512)

    x = jnp.concatenate([x_prompt.reshape(n_prompt, d), x_sample.reshape(dec_b * dec_seq, d)], axis=0)
    mem = mem_prompt.reshape(batch * mem_len, d)
    ck = cache_k.reshape(depth, cache_k.shape[1], PAGE, d_sb)
    cv = cache_v.reshape(depth, cache_v.shape[1], PAGE, d_sb)
    zero_buf = jnp.zeros((batch, HALO, d_conv), F32)
    state_pad = jnp.pad(state_conv, ((0, 0), (0, 0), (HALO - (CONV_W - 1), 0), (0, 0)))

    outs = {k: [] for k in ("kp", "vp", "cp", "mk", "mv", "ks", "vs", "cs")}
    for l in range(depth):
        x = _ffn(x, norm_ffa[l], ffa_w_gate[l], ffa_w_up[l], ffa_w_down[l], norm_final, tm=tm, tf=512, final_norm=False)
        proj = _norm_matmul(x, norm_mix[l], w_in[l], tm=tm, tn=_largest_tile(w_in.shape[2], 512, 128), name="in_proj")

        attn = _sb_prompt(proj, sb_bias[l], batch=batch, seq=seq, heads=heads, tq=tq_attn)
        attn = _sb_sample(attn, proj, ck, cv, page_table, sb_bias[l], layer=l, row0=n_prompt, heads=heads, lq=dec_seq)

        yact, buf_p = _conv(None, proj, zero_buf, conv_w_dw[l], conv_b_dw[l], conv_ln_g[l], conv_ln_b[l],
                            batch=batch, seq=seq, row0=0, tq=tq_conv, val_blk=val_col // d_conv)
        yact, buf_s = _conv(yact, proj, state_pad[l], conv_w_dw[l], conv_b_dw[l], conv_ln_g[l], conv_ln_b[l],
                            batch=dec_b, seq=dec_seq, row0=n_prompt, tq=dec_seq, val_blk=val_col // d_conv)

        x = _mix(x, attn, yact, proj, w_sb_o[l], w_conv_o[l], w_mix_o[l], tm=tm, tn=256,
                 gsb_col=gsb_col, gconv_col=gconv_col)

        mk = _norm_matmul(mem, norm_mem[l], w_ck[l], tm=batch * mem_len, tn=d_mem, name="mem_k")
        mv = _norm_matmul(mem, norm_mem[l], w_cv[l], tm=batch * mem_len, tn=d_mem, name="mem_v")
        x = _cross(x, norm_cross[l], w_cq[l], mk, mv, w_co[l], batch=batch, seq=seq, row0=0, tm=tm_cross, heads=mem_heads)
        x = _cross(x, norm_cross[l], w_cq[l], cache_mem_k[l].reshape(dec_b * mem_len, d_mem),
                   cache_mem_v[l].reshape(dec_b * mem_len, d_mem), w_co[l],
                   batch=dec_b, seq=dec_seq, row0=n_prompt, tm=dec_seq, heads=mem_heads)

        x = _ffn(x, norm_ffb[l], ffb_w_gate[l], ffb_w_up[l], ffb_w_down[l], norm_final, tm=tm, tf=512,
                 final_norm=(l == depth - 1))

        outs["kp"].append(proj[:n_prompt, d_sb:2 * d_sb].reshape(batch, seq, heads, HEAD_DIM))
        outs["vp"].append(proj[:n_prompt, 2 * d_sb:3 * d_sb].reshape(batch, seq, heads, HEAD_DIM))
        outs["cp"].append(buf_p[:, HALO - (CONV_W - 1):])
        outs["mk"].append(mk.reshape(batch, mem_len, mem_heads, HEAD_DIM))
        outs["mv"].append(mv.reshape(batch, mem_len, mem_heads, HEAD_DIM))
        outs["ks"].append(proj[n_prompt:, d_sb:2 * d_sb].reshape(dec_b, dec_seq, heads, HEAD_DIM))
        outs["vs"].append(proj[n_prompt:, 2 * d_sb:3 * d_sb].reshape(dec_b, dec_seq, heads, HEAD_DIM))
        outs["cs"].append(buf_s[:, HALO - (CONV_W - 1):])

    y_prompt = x[:n_prompt].reshape(batch, seq, d)
    y_sample = x[n_prompt:].reshape(dec_b, dec_seq, d)
    return (y_prompt, y_sample, jnp.stack(outs["kp"]), jnp.stack(outs["vp"]), jnp.stack(outs["cp"]),
            jnp.stack(outs["mk"]), jnp.stack(outs["mv"]), jnp.stack(outs["ks"]), jnp.stack(outs["vs"]),
            jnp.stack(outs["cs"]))
```

```python
import functools
import math

import jax
import jax.numpy as jnp
from jax import lax
from jax.experimental import pallas as pl
from jax.experimental.pallas import tpu as pltpu

EPS = 1e-6
PAGE = 128
HEAD_DIM = 128
CONV_W = 31
HALO = 32
VMEM_LIMIT = 56 * 1024 * 1024
LOG2E = math.log2(math.e)

BF16 = jnp.bfloat16
F32 = jnp.float32


def _cparams(n_axes):
    return pltpu.CompilerParams(dimension_semantics=("arbitrary",) * n_axes, vmem_limit_bytes=VMEM_LIMIT)


def _rms(x, g):
    return x * lax.rsqrt(jnp.mean(x * x, axis=-1, keepdims=True) + EPS) * g


def _dot(a, b):
    return jnp.dot(a, b, preferred_element_type=F32)


def _dot_nt(a, b):
    return lax.dot_general(a, b, (((1,), (1,)), ((), ())), preferred_element_type=F32)


def _sigmoid(x):
    return 1.0 / (1.0 + jnp.exp(-x))


def _layer_vec(g):
    return g.reshape(g.shape[0], 1, g.shape[1])


def _ffn_kernel(x_ref, g_ref, wg_ref, wu_ref, wd_ref, gf_ref, o_ref, u_ref, *, nf, tf, tail, final_norm):
    f = pl.program_id(1)

    @pl.when(f == 0)
    def _():
        u_ref[...] = _rms(x_ref[...], g_ref[...]).astype(BF16)
        o_ref[...] = jnp.zeros_like(o_ref)

    def step(width):
        u = u_ref[...]
        gate = _dot(u, wg_ref[:, :width].astype(BF16))
        up = _dot(u, wu_ref[:, :width].astype(BF16))
        h = (gate * _sigmoid(gate) * up).astype(BF16)
        o_ref[...] += _dot(h, wd_ref[:width, :].astype(BF16))

    if tail == tf:
        step(tf)
    else:
        @pl.when(f < nf - 1)
        def _():
            step(tf)

        @pl.when(f == nf - 1)
        def _():
            step(tail)

    @pl.when(f == nf - 1)
    def _():
        y = x_ref[...] + 0.5 * o_ref[...]
        if final_norm:
            y = _rms(y, gf_ref[...])
        o_ref[...] = y


def _ffn(x, g, wg, wu, wd, gfin, *, layer, tm, tf, final_norm):
    m, d = x.shape
    dff = wg.shape[2]
    nf = pl.cdiv(dff, tf)
    tail = dff - (nf - 1) * tf
    kern = functools.partial(_ffn_kernel, nf=nf, tf=tf, tail=tail, final_norm=final_norm)
    return pl.pallas_call(
        kern,
        out_shape=jax.ShapeDtypeStruct((m, d), F32),
        grid=(m // tm, nf),
        in_specs=[
            pl.BlockSpec((tm, d), lambda i, f: (i, 0)),
            pl.BlockSpec((None, 1, d), lambda i, f: (layer, 0, 0)),
            pl.BlockSpec((None, d, tf), lambda i, f: (layer, 0, f)),
            pl.BlockSpec((None, d, tf), lambda i, f: (layer, 0, f)),
            pl.BlockSpec((None, tf, d), lambda i, f: (layer, f, 0)),
            pl.BlockSpec((1, d), lambda i, f: (0, 0)),
        ],
        out_specs=pl.BlockSpec((tm, d), lambda i, f: (i, 0)),
        scratch_shapes=[pltpu.VMEM((tm, d), BF16)],
        compiler_params=_cparams(2),
        name="ffn",
    )(x, _layer_vec(g), wg, wu, wd, gfin.reshape(1, d))


def _norm_matmul_kernel(x_ref, g_ref, w_ref, o_ref, u_ref):
    @pl.when(pl.program_id(1) == 0)
    def _():
        u_ref[...] = _rms(x_ref[...], g_ref[...]).astype(BF16)

    o_ref[...] = _dot(u_ref[...], w_ref[...].astype(BF16))


def _norm_matmul(x, g, w, *, layer, tm, tn, name):
    m, d = x.shape
    n = w.shape[2]
    return pl.pallas_call(
        _norm_matmul_kernel,
        out_shape=jax.ShapeDtypeStruct((m, n), F32),
        grid=(m // tm, n // tn),
        in_specs=[
            pl.BlockSpec((tm, d), lambda i, j: (i, 0)),
            pl.BlockSpec((None, 1, d), lambda i, j: (layer, 0, 0)),
            pl.BlockSpec((None, d, tn), lambda i, j: (layer, 0, j)),
        ],
        out_specs=pl.BlockSpec((tm, tn), lambda i, j: (i, j)),
        scratch_shapes=[pltpu.VMEM((tm, d), BF16)],
        compiler_params=_cparams(2),
        name=name,
    )(x, _layer_vec(g), w)


def _in_proj_kernel(x_ref, g_ref, w_ref, o_ref, qkv_ref, u_ref, *, q_blocks, qkv_blocks, q_scale):
    j = pl.program_id(1)

    @pl.when(j == 0)
    def _():
        u_ref[...] = _rms(x_ref[...], g_ref[...]).astype(BF16)

    acc = _dot(u_ref[...], w_ref[...].astype(BF16))
    o_ref[...] = acc

    @pl.when(j < q_blocks)
    def _():
        qkv_ref[...] = (acc * q_scale).astype(BF16)

    @pl.when((j >= q_blocks) & (j < qkv_blocks))
    def _():
        qkv_ref[...] = acc.astype(BF16)


def _in_proj(x, g, w, *, layer, tm, tn, d_sb, q_scale):
    m, d = x.shape
    n = w.shape[2]
    q_blocks, qkv_blocks = d_sb // tn, 3 * d_sb // tn
    kern = functools.partial(_in_proj_kernel, q_blocks=q_blocks, qkv_blocks=qkv_blocks, q_scale=q_scale)
    return pl.pallas_call(
        kern,
        out_shape=(jax.ShapeDtypeStruct((m, n), F32), jax.ShapeDtypeStruct((m, 3 * d_sb), BF16)),
        grid=(m // tm, n // tn),
        in_specs=[
            pl.BlockSpec((tm, d), lambda i, j: (i, 0)),
            pl.BlockSpec((None, 1, d), lambda i, j: (layer, 0, 0)),
            pl.BlockSpec((None, d, tn), lambda i, j: (layer, 0, j)),
        ],
        out_specs=(
            pl.BlockSpec((tm, tn), lambda i, j: (i, j)),
            pl.BlockSpec((tm, tn), lambda i, j: (i, jnp.minimum(j, qkv_blocks - 1))),
        ),
        scratch_shapes=[pltpu.VMEM((tm, d), BF16)],
        compiler_params=_cparams(2),
        name="in_proj",
    )(x, _layer_vec(g), w)


MASKED = -1e30


def _sb_sums(z, tri2):
    neg_abs = lax.bitcast_convert_type(lax.bitcast_convert_type(z, jnp.uint32) | jnp.uint32(0x80000000), F32)
    soft = jnp.maximum(z, 0.0) + jnp.log(1.0 + jnp.exp2(neg_abs)) * LOG2E
    hi = soft.astype(BF16)
    lo = (soft - hi.astype(F32)).astype(BF16)
    return _dot(jnp.concatenate([hi, lo], axis=1), tri2)


def _sb_weights(z, csum, carry):
    return jnp.exp2(z - csum - carry), carry + csum[:, 0:1]


def _tri2(keys):
    j = lax.broadcasted_iota(jnp.int32, (2 * keys, keys), 0) & (keys - 1)
    s = lax.broadcasted_iota(jnp.int32, (2 * keys, keys), 1)
    return (j >= s).astype(BF16)


def _sbp_kernel(bias_ref, tile_ref, pair_ref, q_ref, k_ref, v_ref, tri_ref, o_ref, z_ref, e_ref, tot_ref,
                *, tq, n_iter):
    bias = bias_ref[pl.program_id(1)]
    tri2 = tri_ref[...]
    wide = 2 * tq
    diff = lax.broadcasted_iota(jnp.int32, (tq, wide), 1) - lax.broadcasted_iota(jnp.int32, (tq, wide), 0)

    def where(t):
        i, p = tile_ref[t], pair_ref[t]
        return i, p, pl.ds(pl.multiple_of(i * tq, tq), tq), pl.ds(pl.multiple_of(p * wide, wide), wide)

    def logits(t):
        i, p, q_rows, k_rows = where(t)
        z = _dot_nt(q_ref[q_rows, :], k_ref[k_rows, :]) + bias
        z_ref[...] = jnp.where(diff < i * tq - p * wide, z, MASKED)

    def sums():
        for s in range(2):
            z = z_ref[:, s * tq:(s + 1) * tq]
            csum = _sb_sums(z, tri2)
            e_ref[:, s * tq:(s + 1) * tq] = z - csum
            tot_ref[s] = csum[:, 0:1]

    def apply(t, carry, acc):
        i, p, q_rows, k_rows = where(t)
        fresh = p == i // 2
        carry = jnp.where(fresh, 0.0, carry)
        acc = jnp.where(fresh, 0.0, acc)
        w1 = jnp.exp2(e_ref[:, tq:] - carry)
        carry = carry + tot_ref[1]
        w0 = jnp.exp2(e_ref[:, :tq] - carry)
        carry = carry + tot_ref[0]
        acc = acc + _dot(jnp.concatenate([w0, w1], axis=1).astype(BF16), v_ref[k_rows, :])
        o_ref[q_rows, :] = acc
        return carry, acc

    state = (jnp.zeros((tq, 1), F32), jnp.zeros((tq, HEAD_DIM), F32))
    logits(0)
    sums()
    logits(1)

    def body(t, st):
        st = apply(t - 2, *st)
        sums()
        logits(t)
        return st

    state = lax.fori_loop(2, n_iter, body, state)
    state = apply(n_iter - 2, *state)
    sums()
    apply(n_iter - 1, *state)


def _sb_prompt(qkv, bias2, *, n_tok, batch, seq, heads, tq):
    nq = seq // tq
    assert nq % 2 == 0, "key blocks are taken in pairs"
    steps = [(i, p) for i in range(nq) for p in range(i // 2, -1, -1)]
    tiles = jnp.asarray([i for i, _ in steps], jnp.int32)
    pairs = jnp.asarray([p for _, p in steps], jnp.int32)
    kern = functools.partial(_sbp_kernel, tq=tq, n_iter=len(steps))
    grid_spec = pltpu.PrefetchScalarGridSpec(
        num_scalar_prefetch=3,
        grid=(batch, heads),
        in_specs=[
            pl.BlockSpec((seq, HEAD_DIM), lambda b, h, *_: (b, h)),
            pl.BlockSpec((seq, HEAD_DIM), lambda b, h, *_: (b, heads + h)),
            pl.BlockSpec((seq, HEAD_DIM), lambda b, h, *_: (b, 2 * heads + h)),
            pl.BlockSpec((2 * tq, tq), lambda b, h, *_: (0, 0)),
        ],
        out_specs=pl.BlockSpec((seq, HEAD_DIM), lambda b, h, *_: (b, h)),
        scratch_shapes=[
            pltpu.VMEM((tq, 2 * tq), F32),
            pltpu.VMEM((tq, 2 * tq), F32),
            pltpu.VMEM((2, tq, 1), F32),
        ],
    )
    return pl.pallas_call(
        kern,
        out_shape=jax.ShapeDtypeStruct((n_tok, heads * HEAD_DIM), F32),
        grid_spec=grid_spec,
        compiler_params=_cparams(2),
        name="sb_prompt",
    )(bias2, tiles, pairs, qkv, qkv, qkv, _tri2(tq))


def _sbs_kernel(pt_ref, q_ref, kn_ref, vn_ref, *rest, heads, lq, pages, n_steps, q_scale):
    kp_refs = rest[:pages]
    vp_refs = rest[pages:2 * pages]
    bias_ref, tri_ref, _, o_ref, carry_ref, acc_ref = rest[2 * pages:]
    j = pl.program_id(1)
    pad = jnp.zeros((lq, HEAD_DIM), F32)

    def q_head(h):
        qh = q_ref[:, h * HEAD_DIM:(h + 1) * HEAD_DIM] * q_scale
        return jnp.concatenate([qh, pad], axis=0).astype(BF16)

    def block(k_head, v_head, mask):
        z = jnp.concatenate([_dot_nt(q_head(h), k_head(h))[0:lq] for h in range(heads)], axis=0) + bias_ref[...]
        if mask is not None:
            z = jnp.where(mask, z, MASKED)
        w, carry = _sb_weights(z, _sb_sums(z, tri_ref[...]), carry_ref[...])
        carry_ref[...] = carry
        for h in range(heads):
            wh = jnp.concatenate([w[h * lq:(h + 1) * lq], pad], axis=0).astype(BF16)
            acc_ref[h * lq:(h + 1) * lq, :] += _dot(wh, v_head(h))[0:lq]

    @pl.when(j == 0)
    def _():
        carry_ref[...] = jnp.zeros_like(carry_ref)
        acc_ref[...] = jnp.zeros_like(acc_ref)
        fill = jnp.zeros((PAGE - lq, HEAD_DIM), F32)
        new = lambda ref: lambda h: jnp.concatenate(
            [ref[:, h * HEAD_DIM:(h + 1) * HEAD_DIM], fill], axis=0).astype(BF16)
        key = lax.broadcasted_iota(jnp.int32, (heads * lq, PAGE), 1)
        t = lax.broadcasted_iota(jnp.int32, (heads * lq, PAGE), 0) & (lq - 1)
        block(new(kn_ref), new(vn_ref), key < t)

    @pl.when(j > 0)
    def _():
        for s in range(pages):
            cached = lambda ref: lambda h: ref[pl.ds(h, PAGE, stride=heads), :].astype(BF16)
            block(cached(kp_refs[s]), cached(vp_refs[s]), None)

    @pl.when(j == n_steps - 1)
    def _():
        for h in range(heads):
            o_ref[:, h * HEAD_DIM:(h + 1) * HEAD_DIM] = acc_ref[h * lq:(h + 1) * lq, :]


def _sb_sample(attn, proj, cache_k, cache_v, page_table, bias2, *, layer, row0, heads, lq, q_scale):
    dec_b, n_pages = page_table.shape
    assert lq & (lq - 1) == 0
    width = heads * HEAD_DIM
    rows = heads * lq
    pages = max(p for p in (8, 4, 2, 1) if n_pages % p == 0)
    n_steps = n_pages // pages + 1
    rb0 = row0 // lq
    kern = functools.partial(_sbs_kernel, heads=heads, lq=lq, pages=pages, n_steps=n_steps, q_scale=q_scale)

    def page(s):
        def index(b, j, pt):
            pos = n_pages - 1 - (jnp.maximum(j, 1) - 1) * pages - s
            return (layer, pt[b, pos], 0, 0)
        return pl.BlockSpec((None, None, PAGE * heads, HEAD_DIM), index)

    row_spec = lambda col: pl.BlockSpec((lq, width), lambda b, j, pt: (rb0 + b, col))
    grid_spec = pltpu.PrefetchScalarGridSpec(
        num_scalar_prefetch=1,
        grid=(dec_b, n_steps),
        in_specs=[row_spec(0), row_spec(1), row_spec(2)]
        + [page(s) for s in range(pages)] + [page(s) for s in range(pages)]
        + [
            pl.BlockSpec((rows, PAGE), lambda b, j, pt: (0, 0)),
            pl.BlockSpec((2 * PAGE, PAGE), lambda b, j, pt: (0, 0)),
            pl.BlockSpec(memory_space=pl.ANY),
        ],
        out_specs=row_spec(0),
        scratch_shapes=[pltpu.VMEM((rows, 1), F32), pltpu.VMEM((rows, HEAD_DIM), F32)],
    )
    bias_rows = jnp.broadcast_to(jnp.repeat(bias2, lq)[:, None], (rows, PAGE))
    n_in = 1 + 3 + 2 * pages + 3
    return pl.pallas_call(
        kern,
        out_shape=jax.ShapeDtypeStruct(attn.shape, attn.dtype),
        grid_spec=grid_spec,
        input_output_aliases={n_in - 1: 0},
        compiler_params=_cparams(2),
        name="sb_sample",
    )(page_table, proj, proj, proj, *([cache_k] * pages), *([cache_v] * pages), bias_rows, _tri2(PAGE), attn)


def _conv_kernel(cv_ref, cg_ref, init_ref, w_ref, b_ref, lg_ref, lb_ref, *rest, tq, nt, rc, cc):
    y_ref, buf_ref, ext_ref, acc_ref = rest[-4:]
    i = pl.program_id(1)
    ch = cv_ref.shape[1]

    @pl.when(i == 0)
    def _():
        ext_ref[0:HALO, :] = init_ref[...]

    ext_ref[HALO:HALO + tq, :] = cv_ref[...] * _sigmoid(cg_ref[...])
    first = HALO - (CONV_W - 1)
    for r in range(0, tq, rc):
        for c in range(0, ch, cc):
            acc = jnp.broadcast_to(b_ref[:, c:c + cc], (rc, cc))
            for j in range(CONV_W):
                acc = acc + ext_ref[r + first + j:r + first + j + rc, c:c + cc] * w_ref[j:j + 1, c:c + cc]
            acc_ref[r:r + rc, c:c + cc] = acc
    y = acc_ref[...]
    mu = jnp.mean(y, axis=-1, keepdims=True)
    yc = y - mu
    y = yc * lax.rsqrt(jnp.mean(yc * yc, axis=-1, keepdims=True) + EPS) * lg_ref[...] + lb_ref[...]
    y_ref[...] = (y * _sigmoid(y)).astype(y_ref.dtype)

    @pl.when(i == nt - 1)
    def _():
        buf_ref[...] = ext_ref[tq:tq + HALO, :]

    if nt > 1:
        @pl.when(i < nt - 1)
        def _():
            ext_ref[0:HALO, :] = ext_ref[tq:tq + HALO, :]


def _conv(yact, proj, init, w_dw, b_dw, ln_g, ln_b, *, layer, batch, seq, row0, tq, val_blk):
    ch = w_dw.shape[2]
    aliased = yact is not None
    nt = seq // tq
    rb0 = row0 // tq
    rc = min(tq, 64)
    kern = functools.partial(_conv_kernel, tq=tq, nt=nt, rc=rc, cc=256)
    row_map = lambda b, i: (rb0 + b * nt + i, 0)
    vec = pl.BlockSpec((None, 1, ch), lambda b, i: (layer, 0, 0))
    in_specs = [
        pl.BlockSpec((tq, ch), lambda b, i: (rb0 + b * nt + i, val_blk)),
        pl.BlockSpec((tq, ch), lambda b, i: (rb0 + b * nt + i, val_blk + 1)),
        pl.BlockSpec((None, HALO, ch), lambda b, i: (b, 0, 0)),
        pl.BlockSpec((None, CONV_W, ch), lambda b, i: (layer, 0, 0)),
        vec, vec, vec,
    ]
    args = [proj, proj, init, w_dw, _layer_vec(b_dw), _layer_vec(ln_g), _layer_vec(ln_b)]
    if aliased:
        in_specs.append(pl.BlockSpec(memory_space=pl.ANY))
        args.append(yact)
    return pl.pallas_call(
        kern,
        out_shape=(jax.ShapeDtypeStruct((proj.shape[0], ch), F32), jax.ShapeDtypeStruct((batch, HALO, ch), F32)),
        grid=(batch, nt),
        in_specs=in_specs,
        out_specs=(pl.BlockSpec((tq, ch), row_map), pl.BlockSpec((None, HALO, ch), lambda b, i: (b, 0, 0))),
        scratch_shapes=[pltpu.VMEM((HALO + tq, ch), F32), pltpu.VMEM((tq, ch), F32)],
        input_output_aliases={len(args) - 1: 0} if aliased else {},
        compiler_params=_cparams(2),
        name="conv_sample" if aliased else "conv_prompt",
    )(*args)


def _mix_kernel(x_ref, a_ref, y_ref, gs_ref, gc_ref, wsb_ref, wcv_ref, wmx_ref, o_ref, *, nn):
    n = pl.program_id(1)

    @pl.when(n == 0)
    def _():
        o_ref[...] = jnp.zeros_like(o_ref)

    o_sb = _dot(a_ref[...].astype(BF16), wsb_ref[...].astype(BF16))
    o_cv = _dot(y_ref[...].astype(BF16), wcv_ref[...].astype(BF16))
    mixed = _sigmoid(gs_ref[...]) * o_sb + _sigmoid(gc_ref[...]) * o_cv
    o_ref[...] += _dot(mixed.astype(BF16), wmx_ref[...].astype(BF16))

    @pl.when(n == nn - 1)
    def _():
        o_ref[...] = x_ref[...] + o_ref[...]


def _mix(x, attn, yact, proj, w_sb_o, w_conv_o, w_mix_o, *, layer, tm, tn, gsb_col, gconv_col):
    m, d = x.shape
    d_sb = attn.shape[1]
    d_conv = yact.shape[1]
    nn = d // tn
    kern = functools.partial(_mix_kernel, nn=nn)
    return pl.pallas_call(
        kern,
        out_shape=jax.ShapeDtypeStruct((m, d), F32),
        grid=(m // tm, nn),
        in_specs=[
            pl.BlockSpec((tm, d), lambda i, n: (i, 0)),
            pl.BlockSpec((tm, d_sb), lambda i, n: (i, 0)),
            pl.BlockSpec((tm, d_conv), lambda i, n: (i, 0)),
            pl.BlockSpec((tm, tn), lambda i, n: (i, gsb_col // tn + n)),
            pl.BlockSpec((tm, tn), lambda i, n: (i, gconv_col // tn + n)),
            pl.BlockSpec((None, d_sb, tn), lambda i, n: (layer, 0, n)),
            pl.BlockSpec((None, d_conv, tn), lambda i, n: (layer, 0, n)),
            pl.BlockSpec((None, tn, d), lambda i, n: (layer, n, 0)),
        ],
        out_specs=pl.BlockSpec((tm, d), lambda i, n: (i, 0)),
        compiler_params=_cparams(2),
        name="mix",
    )(x, attn, yact, proj, proj, w_sb_o, w_conv_o, w_mix_o)


def _cross_kernel(x_ref, g_ref, wq_ref, mk_ref, mv_ref, wo_ref, o_ref, *, heads, scale):
    x = x_ref[...]
    u = _rms(x, g_ref[...]).astype(BF16)
    q = _dot(u, wq_ref[...].astype(BF16))
    outs = []
    for h in range(heads):
        sl = slice(h * HEAD_DIM, (h + 1) * HEAD_DIM)
        s = _dot_nt(q[:, sl].astype(BF16), mk_ref[:, sl].astype(BF16)) * scale
        p = jnp.exp(s - jnp.max(s, axis=-1, keepdims=True))
        o = _dot(p.astype(BF16), mv_ref[:, sl].astype(BF16)) / jnp.sum(p, axis=-1, keepdims=True)
        outs.append(o.astype(BF16))
    o_ref[...] = x + _dot(jnp.concatenate(outs, axis=1), wo_ref[...].astype(BF16))


def _cross(x, g, w_cq, mem_k, mem_v, w_co, *, layer, batch, seq, row0, tm, heads):
    m, d = x.shape
    d_mem = w_cq.shape[2]
    mem_len = mem_k.shape[0] // batch
    nt = seq // tm
    rb0 = row0 // tm
    kern = functools.partial(_cross_kernel, heads=heads, scale=HEAD_DIM ** -0.5)
    return pl.pallas_call(
        kern,
        out_shape=jax.ShapeDtypeStruct((m, d), F32),
        grid=(batch, nt),
        in_specs=[
            pl.BlockSpec((tm, d), lambda b, i: (rb0 + b * nt + i, 0)),
            pl.BlockSpec((None, 1, d), lambda b, i: (layer, 0, 0)),
            pl.BlockSpec((None, d, d_mem), lambda b, i: (layer, 0, 0)),
            pl.BlockSpec((mem_len, d_mem), lambda b, i: (b, 0)),
            pl.BlockSpec((mem_len, d_mem), lambda b, i: (b, 0)),
            pl.BlockSpec((None, d_mem, d), lambda b, i: (layer, 0, 0)),
        ],
        out_specs=pl.BlockSpec((tm, d), lambda b, i: (rb0 + b * nt + i, 0)),
        input_output_aliases={0: 0},
        compiler_params=_cparams(2),
        name="cross",
    )(x, _layer_vec(g), w_cq, mem_k, mem_v, w_co)


def _largest_tile(n, cap, mult):
    best = None
    for t in range(mult, cap + 1, mult):
        if n % t == 0:
            best = t
    assert best is not None, (n, cap, mult)
    return best


def kernel(x_prompt, x_sample, mem_prompt, cache_k, cache_v, state_conv, cache_mem_k, cache_mem_v, page_table, norm_ffa, ffa_w_gate, ffa_w_up, ffa_w_down, norm_mix, w_in, sb_bias, conv_w_dw, conv_b_dw, conv_ln_g, conv_ln_b, w_conv_o, w_sb_o, w_mix_o, norm_cross, norm_mem, w_cq, w_ck, w_cv, w_co, norm_ffb, ffb_w_gate, ffb_w_up, ffb_w_down, norm_final):
    batch, seq, d = x_prompt.shape
    dec_b, dec_seq, _ = x_sample.shape
    depth = w_in.shape[0]
    heads = cache_k.shape[3]
    d_sb = heads * HEAD_DIM
    d_conv = conv_w_dw.shape[2]
    mem_len = mem_prompt.shape[1]
    mem_heads = cache_mem_k.shape[3]
    d_mem = mem_heads * HEAD_DIM
    n_prompt = batch * seq
    n_tok = n_prompt + dec_b * dec_seq
    val_col, gsb_col, gconv_col = 3 * d_sb, 3 * d_sb + 2 * d_conv, 3 * d_sb + 2 * d_conv + d
    assert d_conv == d_sb and val_col % d_conv == 0

    tm = _largest_tile(n_tok, 768, 16)
    tn_in = _largest_tile(math.gcd(w_in.shape[2], d_sb), 512, 128)
    tq_attn = min(seq, 256)
    tq_conv = min(seq, 256)
    tm_cross = min(seq, 512)
    q_scale = HEAD_DIM ** -0.5 * LOG2E
    bias2 = sb_bias * LOG2E

    x = jnp.concatenate([x_prompt.reshape(n_prompt, d), x_sample.reshape(dec_b * dec_seq, d)], axis=0)
    mem = mem_prompt.reshape(batch * mem_len, d)
    ck = cache_k.reshape(depth, cache_k.shape[1], PAGE * heads, HEAD_DIM)
    cv = cache_v.reshape(depth, cache_v.shape[1], PAGE * heads, HEAD_DIM)
    zero_buf = jnp.zeros((batch, HALO, d_conv), F32)
    state_pad = jnp.pad(state_conv, ((0, 0), (0, 0), (HALO - (CONV_W - 1), 0), (0, 0)))

    outs = {k: [] for k in ("kp", "vp", "cp", "mk", "mv", "ks", "vs", "cs")}
    for l in range(depth):
        x = _ffn(x, norm_ffa, ffa_w_gate, ffa_w_up, ffa_w_down, norm_final, layer=l, tm=tm, tf=512, final_norm=False)
        proj, qkv = _in_proj(x, norm_mix, w_in, layer=l, tm=tm, tn=tn_in, d_sb=d_sb, q_scale=q_scale)

        attn = _sb_prompt(qkv, bias2[l], n_tok=n_tok, batch=batch, seq=seq, heads=heads, tq=tq_attn)
        attn = _sb_sample(attn, proj, ck, cv, page_table, bias2[l], layer=l, row0=n_prompt, heads=heads, lq=dec_seq,
                          q_scale=q_scale)

        yact, buf_p = _conv(None, proj, zero_buf, conv_w_dw, conv_b_dw, conv_ln_g, conv_ln_b, layer=l,
                            batch=batch, seq=seq, row0=0, tq=tq_conv, val_blk=val_col // d_conv)
        yact, buf_s = _conv(yact, proj, state_pad[l], conv_w_dw, conv_b_dw, conv_ln_g, conv_ln_b, layer=l,
                            batch=dec_b, seq=dec_seq, row0=n_prompt, tq=dec_seq, val_blk=val_col // d_conv)

        x = _mix(x, attn, yact, proj, w_sb_o, w_conv_o, w_mix_o, layer=l, tm=tm, tn=256,
                 gsb_col=gsb_col, gconv_col=gconv_col)

        mk = _norm_matmul(mem, norm_mem, w_ck, layer=l, tm=batch * mem_len, tn=d_mem, name="mem_k")
        mv = _norm_matmul(mem, norm_mem, w_cv, layer=l, tm=batch * mem_len, tn=d_mem, name="mem_v")
        x = _cross(x, norm_cross, w_cq, mk, mv, w_co, layer=l, batch=batch, seq=seq, row0=0, tm=tm_cross,
                   heads=mem_heads)
        x = _cross(x, norm_cross, w_cq, cache_mem_k[l].reshape(dec_b * mem_len, d_mem),
                   cache_mem_v[l].reshape(dec_b * mem_len, d_mem), w_co, layer=l,
                   batch=dec_b, seq=dec_seq, row0=n_prompt, tm=dec_seq, heads=mem_heads)

        x = _ffn(x, norm_ffb, ffb_w_gate, ffb_w_up, ffb_w_down, norm_final, layer=l, tm=tm, tf=512,
                 final_norm=(l == depth - 1))

        outs["kp"].append(proj[:n_prompt, d_sb:2 * d_sb].reshape(batch, seq, heads, HEAD_DIM))
        outs["vp"].append(proj[:n_prompt, 2 * d_sb:3 * d_sb].reshape(batch, seq, heads, HEAD_DIM))
        outs["cp"].append(buf_p[:, HALO - (CONV_W - 1):])
        outs["mk"].append(mk.reshape(batch, mem_len, mem_heads, HEAD_DIM))
        outs["mv"].append(mv.reshape(batch, mem_len, mem_heads, HEAD_DIM))
        outs["ks"].append(proj[n_prompt:, d_sb:2 * d_sb].reshape(dec_b, dec_seq, heads, HEAD_DIM))
        outs["vs"].append(proj[n_prompt:, 2 * d_sb:3 * d_sb].reshape(dec_b, dec_seq, heads, HEAD_DIM))
        outs["cs"].append(buf_s[:, HALO - (CONV_W - 1):])

    y_prompt = x[:n_prompt].reshape(batch, seq, d)
    y_sample = x[n_prompt:].reshape(dec_b, dec_seq, d)
    return (y_prompt, y_sample, jnp.stack(outs["kp"]), jnp.stack(outs["vp"]), jnp.stack(outs["cp"]),
            jnp.stack(outs["mk"]), jnp.stack(outs["mv"]), jnp.stack(outs["ks"]), jnp.stack(outs["vs"]),
            jnp.stack(outs["cs"]))
```

```python
import functools
import math

import jax
import jax.numpy as jnp
from jax import lax
from jax.experimental import pallas as pl
from jax.experimental.pallas import tpu as pltpu

EPS = 1e-6
PAGE = 128
HEAD_DIM = 128
SUBLANES = 8
CONV_W = 31
HALO = 32
VMEM_LIMIT = 60 * 1024 * 1024
LOG2E = math.log2(math.e)

BF16 = jnp.bfloat16
F32 = jnp.float32


def _cparams(n_axes):
    return pltpu.CompilerParams(dimension_semantics=("arbitrary",) * n_axes, vmem_limit_bytes=VMEM_LIMIT)


def _rms(x, g):
    return x * lax.rsqrt(jnp.mean(x * x, axis=-1, keepdims=True) + EPS) * g


def _dot(a, b):
    return jnp.dot(a, b, preferred_element_type=F32)


def _dot_nt(a, b):
    return lax.dot_general(a, b, (((1,), (1,)), ((), ())), preferred_element_type=F32)


def _sigmoid(x):
    return 1.0 / (1.0 + jnp.exp(-x))


def _layer_vec(g):
    return g.reshape(g.shape[0], 1, g.shape[1])


def _ffn_kernel(x_ref, g_ref, wg_ref, wu_ref, wd_ref, gf_ref, o_ref, u_ref, *, nf, tf, tail, final_norm):
    f = pl.program_id(1)

    @pl.when(f == 0)
    def _():
        u_ref[...] = _rms(x_ref[...], g_ref[...]).astype(BF16)
        o_ref[...] = jnp.zeros_like(o_ref)

    def step(width):
        u = u_ref[...]
        gate = _dot(u, wg_ref[:, :width].astype(BF16))
        up = _dot(u, wu_ref[:, :width].astype(BF16))
        h = (gate * _sigmoid(gate) * up).astype(BF16)
        o_ref[...] += _dot(h, wd_ref[:width, :].astype(BF16))

    if tail == tf:
        step(tf)
    else:
        @pl.when(f < nf - 1)
        def _():
            step(tf)

        @pl.when(f == nf - 1)
        def _():
            step(tail)

    @pl.when(f == nf - 1)
    def _():
        y = x_ref[...] + 0.5 * o_ref[...]
        if final_norm:
            y = _rms(y, gf_ref[...])
        o_ref[...] = y


def _ffn(x, g, wg, wu, wd, gfin, *, layer, tm, tf, final_norm):
    m, d = x.shape
    dff = wg.shape[2]
    nf = pl.cdiv(dff, tf)
    tail = dff - (nf - 1) * tf
    kern = functools.partial(_ffn_kernel, nf=nf, tf=tf, tail=tail, final_norm=final_norm)
    return pl.pallas_call(
        kern,
        out_shape=jax.ShapeDtypeStruct((m, d), F32),
        grid=(m // tm, nf),
        in_specs=[
            pl.BlockSpec((tm, d), lambda i, f: (i, 0), pipeline_mode=pl.Buffered(1)),
            pl.BlockSpec((None, 1, d), lambda i, f: (layer, 0, 0)),
            pl.BlockSpec((None, d, tf), lambda i, f: (layer, 0, f)),
            pl.BlockSpec((None, d, tf), lambda i, f: (layer, 0, f)),
            pl.BlockSpec((None, tf, d), lambda i, f: (layer, f, 0)),
            pl.BlockSpec((1, d), lambda i, f: (0, 0)),
        ],
        out_specs=pl.BlockSpec((tm, d), lambda i, f: (i, 0), pipeline_mode=pl.Buffered(1)),
        scratch_shapes=[pltpu.VMEM((tm, d), BF16)],
        compiler_params=_cparams(2),
        name="ffn",
    )(x, _layer_vec(g), wg, wu, wd, gfin.reshape(1, d))


def _norm_matmul_kernel(x_ref, g_ref, w_ref, o_ref, u_ref):
    @pl.when(pl.program_id(1) == 0)
    def _():
        u_ref[...] = _rms(x_ref[...], g_ref[...]).astype(BF16)

    o_ref[...] = _dot(u_ref[...], w_ref[...].astype(BF16))


def _norm_matmul(x, g, w, *, layer, tm, tn, name):
    m, d = x.shape
    n = w.shape[2]
    return pl.pallas_call(
        _norm_matmul_kernel,
        out_shape=jax.ShapeDtypeStruct((m, n), F32),
        grid=(m // tm, n // tn),
        in_specs=[
            pl.BlockSpec((tm, d), lambda i, j: (i, 0)),
            pl.BlockSpec((None, 1, d), lambda i, j: (layer, 0, 0)),
            pl.BlockSpec((None, d, tn), lambda i, j: (layer, 0, j)),
        ],
        out_specs=pl.BlockSpec((tm, tn), lambda i, j: (i, j)),
        scratch_shapes=[pltpu.VMEM((tm, d), BF16)],
        compiler_params=_cparams(2),
        name=name,
    )(x, _layer_vec(g), w)


def _in_proj_kernel(x_ref, g_ref, w_ref, o_ref, qkv_ref, u_ref, *, q_blocks, qkv_blocks, q_scale):
    j = pl.program_id(1)

    @pl.when(j == 0)
    def _():
        u_ref[...] = _rms(x_ref[...], g_ref[...]).astype(BF16)

    acc = _dot(u_ref[...], w_ref[...].astype(BF16))
    o_ref[...] = acc

    @pl.when(j < q_blocks)
    def _():
        qkv_ref[...] = (acc * q_scale).astype(BF16)

    @pl.when((j >= q_blocks) & (j < qkv_blocks))
    def _():
        qkv_ref[...] = acc.astype(BF16)


def _in_proj(x, g, w, *, layer, tm, tn, d_sb, q_scale):
    m, d = x.shape
    n = w.shape[2]
    q_blocks, qkv_blocks = d_sb // tn, 3 * d_sb // tn
    kern = functools.partial(_in_proj_kernel, q_blocks=q_blocks, qkv_blocks=qkv_blocks, q_scale=q_scale)
    return pl.pallas_call(
        kern,
        out_shape=(jax.ShapeDtypeStruct((m, n), F32), jax.ShapeDtypeStruct((m, 3 * d_sb), BF16)),
        grid=(m // tm, n // tn),
        in_specs=[
            pl.BlockSpec((tm, d), lambda i, j: (i, 0)),
            pl.BlockSpec((None, 1, d), lambda i, j: (layer, 0, 0)),
            pl.BlockSpec((None, d, tn), lambda i, j: (layer, 0, j)),
        ],
        out_specs=(
            pl.BlockSpec((tm, tn), lambda i, j: (i, j)),
            pl.BlockSpec((tm, tn), lambda i, j: (i, jnp.minimum(j, qkv_blocks - 1))),
        ),
        scratch_shapes=[pltpu.VMEM((tm, d), BF16)],
        compiler_params=_cparams(2),
        name="in_proj",
    )(x, _layer_vec(g), w)


MASKED = -1e30


def _sb_sums(z, tri):
    neg_abs = lax.bitcast_convert_type(lax.bitcast_convert_type(z, jnp.uint32) | jnp.uint32(0x80000000), F32)
    soft = jnp.maximum(z, 0.0) + jnp.log(1.0 + jnp.exp2(neg_abs)) * LOG2E
    parts = []
    for _ in range(tri.shape[0] // z.shape[1]):
        parts.append(soft.astype(BF16))
        soft = soft - parts[-1].astype(F32)
    return _dot(parts[0] if len(parts) == 1 else jnp.concatenate(parts, axis=1), tri)


def _sb_weights(z, csum, carry):
    return jnp.exp2(z - csum - carry), carry + csum[:, 0:1]


def _tri(keys, parts):
    j = lax.broadcasted_iota(jnp.int32, (parts * keys, keys), 0) & (keys - 1)
    s = lax.broadcasted_iota(jnp.int32, (parts * keys, keys), 1)
    return (j >= s).astype(BF16)


def _sbp_kernel(bias_ref, tile_ref, pair_ref, kind_ref, q_ref, k_ref, v_ref, tri_ref, o_ref,
                z_ref, e_ref, tot_ref, mb_ref, *, tq, n_iter):
    bias = bias_ref[pl.program_id(1)]
    tri = tri_ref[...]
    wide = 2 * tq
    diff = lax.broadcasted_iota(jnp.int32, (tq, wide), 1) - lax.broadcasted_iota(jnp.int32, (tq, wide), 0)
    for kind, bound in enumerate((wide, 0, tq)):
        mb_ref[kind] = jnp.where(diff < bound, bias, MASKED)

    def where(t):
        i, p = tile_ref[t], pair_ref[t]
        return i, p, pl.ds(pl.multiple_of(i * tq, tq), tq), pl.ds(pl.multiple_of(p * wide, wide), wide)

    def logits(t):
        _, _, q_rows, k_rows = where(t)
        z_ref[...] = _dot_nt(q_ref[q_rows, :], k_ref[k_rows, :]) + mb_ref[kind_ref[t]]

    def sums():
        for s in range(2):
            z = z_ref[:, s * tq:(s + 1) * tq]
            csum = _sb_sums(z, tri)
            e_ref[:, s * tq:(s + 1) * tq] = z - csum
            tot_ref[s] = csum[:, 0:1]

    def apply(t, carry, acc):
        i, p, q_rows, k_rows = where(t)
        fresh = p == i // 2
        carry = jnp.where(fresh, 0.0, carry)
        acc = jnp.where(fresh, 0.0, acc)
        w1 = jnp.exp2(e_ref[:, tq:] - carry)
        carry = carry + tot_ref[1]
        w0 = jnp.exp2(e_ref[:, :tq] - carry)
        carry = carry + tot_ref[0]
        acc = acc + _dot(jnp.concatenate([w0, w1], axis=1).astype(BF16), v_ref[k_rows, :])
        o_ref[q_rows, :] = acc
        return carry, acc

    state = (jnp.zeros((tq, 1), F32), jnp.zeros((tq, HEAD_DIM), F32))
    logits(0)
    sums()
    logits(1)

    def body(t, st):
        st = apply(t - 2, *st)
        sums()
        logits(t)
        return st

    state = lax.fori_loop(2, n_iter, body, state, unroll=5 if (n_iter - 2) % 5 == 0 else 2)
    state = apply(n_iter - 2, *state)
    sums()
    apply(n_iter - 1, *state)


def _sb_prompt(qkv, bias2, *, n_tok, batch, seq, heads, tq):
    nq = seq // tq
    assert nq % 2 == 0, "key blocks are taken in pairs"
    steps = [(i, p) for i in range(nq) for p in range(i // 2, -1, -1)]
    tiles = jnp.asarray([i for i, _ in steps], jnp.int32)
    pairs = jnp.asarray([p for _, p in steps], jnp.int32)
    kinds = jnp.asarray([0 if p < i // 2 else 1 + i % 2 for i, p in steps], jnp.int32)
    kern = functools.partial(_sbp_kernel, tq=tq, n_iter=len(steps))
    grid_spec = pltpu.PrefetchScalarGridSpec(
        num_scalar_prefetch=4,
        grid=(batch, heads),
        in_specs=[
            pl.BlockSpec((seq, HEAD_DIM), lambda b, h, *_: (b, h)),
            pl.BlockSpec((seq, HEAD_DIM), lambda b, h, *_: (b, heads + h)),
            pl.BlockSpec((seq, HEAD_DIM), lambda b, h, *_: (b, 2 * heads + h)),
            pl.BlockSpec((tq, tq), lambda b, h, *_: (0, 0)),
        ],
        out_specs=pl.BlockSpec((seq, HEAD_DIM), lambda b, h, *_: (b, h)),
        scratch_shapes=[
            pltpu.VMEM((tq, 2 * tq), F32),
            pltpu.VMEM((tq, 2 * tq), F32),
            pltpu.VMEM((2, tq, 1), F32),
            pltpu.VMEM((3, tq, 2 * tq), F32),
        ],
    )
    return pl.pallas_call(
        kern,
        out_shape=jax.ShapeDtypeStruct((n_tok, heads * HEAD_DIM), F32),
        grid_spec=grid_spec,
        compiler_params=_cparams(2),
        name="sb_prompt",
    )(bias2, tiles, pairs, kinds, qkv, qkv, qkv, _tri(tq, 1))


def _sbs_kernel(pt_ref, q_ref, kn_ref, vn_ref, *rest, heads, lq, pages, n_steps, q_scale):
    kp_refs = rest[:pages]
    vp_refs = rest[pages:2 * pages]
    bias_ref, tri_ref, _, o_ref, carry_ref, acc_ref = rest[2 * pages:]
    j = pl.program_id(1)
    pad = jnp.zeros((lq, HEAD_DIM), F32)

    def q_head(h):
        qh = q_ref[:, h * HEAD_DIM:(h + 1) * HEAD_DIM] * q_scale
        return jnp.concatenate([qh, pad], axis=0).astype(BF16)

    def attend(k_heads, v_heads, mask):
        rows = heads * lq
        z = jnp.concatenate(
            [jnp.concatenate([_dot_nt(q_head(h), k_head(h))[0:lq] for h in range(heads)], axis=0) + bias_ref[...]
             for k_head in k_heads], axis=0)
        if mask is not None:
            z = jnp.where(mask, z, MASKED)
        csum = _sb_sums(z, tri_ref[...])
        carry = carry_ref[...]
        acc = [acc_ref[h * lq:(h + 1) * lq, :] for h in range(heads)]
        for s, v_head in enumerate(v_heads):
            w, carry = _sb_weights(z[s * rows:(s + 1) * rows], csum[s * rows:(s + 1) * rows], carry)
            for h in range(heads):
                wh = jnp.concatenate([w[h * lq:(h + 1) * lq], pad], axis=0).astype(BF16)
                acc[h] = acc[h] + _dot(wh, v_head(h))[0:lq]
        carry_ref[...] = carry
        acc_ref[...] = jnp.concatenate(acc, axis=0)

    @pl.when(j == 0)
    def _():
        carry_ref[...] = jnp.zeros_like(carry_ref)
        acc_ref[...] = jnp.zeros_like(acc_ref)
        fill = jnp.zeros((PAGE - lq, HEAD_DIM), F32)
        new = lambda ref: lambda h: jnp.concatenate(
            [ref[:, h * HEAD_DIM:(h + 1) * HEAD_DIM], fill], axis=0).astype(BF16)
        key = lax.broadcasted_iota(jnp.int32, (heads * lq, PAGE), 1)
        t = lax.broadcasted_iota(jnp.int32, (heads * lq, PAGE), 0) & (lq - 1)
        attend([new(kn_ref)], [new(vn_ref)], key < t)

    @pl.when(j > 0)
    def _():
        cached = lambda ref: lambda h: ref[pl.ds(h, PAGE, stride=heads), :].astype(BF16)
        attend([cached(r) for r in kp_refs], [cached(r) for r in vp_refs], None)

    @pl.when(j == n_steps - 1)
    def _():
        for h in range(heads):
            o_ref[:, h * HEAD_DIM:(h + 1) * HEAD_DIM] = acc_ref[h * lq:(h + 1) * lq, :]


def _sb_sample(attn, proj, cache_k, cache_v, page_table, bias2, *, layer, row0, heads, lq, q_scale):
    dec_b, n_pages = page_table.shape
    assert lq & (lq - 1) == 0
    width = heads * HEAD_DIM
    rows = heads * lq
    pages = max(p for p in (8, 4, 2, 1) if n_pages % p == 0)
    n_steps = n_pages // pages + 1
    rb0 = row0 // lq
    kern = functools.partial(_sbs_kernel, heads=heads, lq=lq, pages=pages, n_steps=n_steps, q_scale=q_scale)

    def page(s):
        def index(b, j, pt):
            pos = n_pages - 1 - (jnp.maximum(j, 1) - 1) * pages - s
            return (layer, pt[b, pos], 0, 0)
        return pl.BlockSpec((None, None, PAGE * heads, HEAD_DIM), index)

    row_spec = lambda col: pl.BlockSpec((lq, width), lambda b, j, pt: (rb0 + b, col))
    grid_spec = pltpu.PrefetchScalarGridSpec(
        num_scalar_prefetch=1,
        grid=(dec_b, n_steps),
        in_specs=[row_spec(0), row_spec(1), row_spec(2)]
        + [page(s) for s in range(pages)] + [page(s) for s in range(pages)]
        + [
            pl.BlockSpec((rows, PAGE), lambda b, j, pt: (0, 0)),
            pl.BlockSpec((2 * PAGE, PAGE), lambda b, j, pt: (0, 0)),
            pl.BlockSpec(memory_space=pl.ANY),
        ],
        out_specs=row_spec(0),
        scratch_shapes=[pltpu.VMEM((rows, 1), F32), pltpu.VMEM((rows, HEAD_DIM), F32)],
    )
    bias_rows = jnp.broadcast_to(jnp.repeat(bias2, lq)[:, None], (rows, PAGE))
    n_in = 1 + 3 + 2 * pages + 3
    return pl.pallas_call(
        kern,
        out_shape=jax.ShapeDtypeStruct(attn.shape, attn.dtype),
        grid_spec=grid_spec,
        input_output_aliases={n_in - 1: 0},
        compiler_params=_cparams(2),
        name="sb_sample",
    )(page_table, proj, proj, proj, *([cache_k] * pages), *([cache_v] * pages), bias_rows, _tri(PAGE, 2), attn)


def _conv_kernel(cv_ref, cg_ref, init_ref, w_ref, b_ref, lg_ref, lb_ref, *rest, tq, nt, rc, cc):
    y_ref, buf_ref, ext_ref, rot_ref, acc_ref = rest[-5:]
    i = pl.program_id(1)
    ch = cv_ref.shape[1]

    @pl.when(i == 0)
    def _():
        ext_ref[0:HALO, :] = init_ref[...]
        ext_ref[HALO + tq:HALO + tq + SUBLANES, :] = jnp.zeros((SUBLANES, ch), F32)

    ext_ref[HALO:HALO + tq, :] = cv_ref[...] * _sigmoid(cg_ref[...])
    first = HALO - (CONV_W - 1)
    span = rot_ref.shape[1]
    for s in range(SUBLANES):
        rot_ref[s] = ext_ref[first + s:first + s + span, :]
    for r in range(0, tq, rc):
        for c in range(0, ch, cc):
            acc = jnp.broadcast_to(b_ref[:, c:c + cc], (rc, cc))
            for j in range(CONV_W):
                a, s = divmod(j, SUBLANES)
                row = r + SUBLANES * a
                acc = acc + rot_ref[s, row:row + rc, c:c + cc] * w_ref[j:j + 1, c:c + cc]
            acc_ref[r:r + rc, c:c + cc] = acc
    y = acc_ref[...]
    mu = jnp.mean(y, axis=-1, keepdims=True)
    yc = y - mu
    y = yc * lax.rsqrt(jnp.mean(yc * yc, axis=-1, keepdims=True) + EPS) * lg_ref[...] + lb_ref[...]
    y_ref[...] = (y * _sigmoid(y)).astype(y_ref.dtype)

    @pl.when(i == nt - 1)
    def _():
        buf_ref[...] = ext_ref[tq:tq + HALO, :]

    if nt > 1:
        @pl.when(i < nt - 1)
        def _():
            ext_ref[0:HALO, :] = ext_ref[tq:tq + HALO, :]


def _conv(yact, proj, init, w_dw, b_dw, ln_g, ln_b, *, layer, batch, seq, row0, tq, val_blk):
    ch = w_dw.shape[2]
    aliased = yact is not None
    nt = seq // tq
    rb0 = row0 // tq
    rc = min(tq, 64)
    kern = functools.partial(_conv_kernel, tq=tq, nt=nt, rc=rc, cc=256)
    row_map = lambda b, i: (rb0 + b * nt + i, 0)
    vec = pl.BlockSpec((None, 1, ch), lambda b, i: (layer, 0, 0))
    in_specs = [
        pl.BlockSpec((tq, ch), lambda b, i: (rb0 + b * nt + i, val_blk)),
        pl.BlockSpec((tq, ch), lambda b, i: (rb0 + b * nt + i, val_blk + 1)),
        pl.BlockSpec((None, HALO, ch), lambda b, i: (b, 0, 0)),
        pl.BlockSpec((None, CONV_W, ch), lambda b, i: (layer, 0, 0)),
        vec, vec, vec,
    ]
    args = [proj, proj, init, w_dw, _layer_vec(b_dw), _layer_vec(ln_g), _layer_vec(ln_b)]
    if aliased:
        in_specs.append(pl.BlockSpec(memory_space=pl.ANY))
        args.append(yact)
    return pl.pallas_call(
        kern,
        out_shape=(jax.ShapeDtypeStruct((proj.shape[0], ch), F32), jax.ShapeDtypeStruct((batch, HALO, ch), F32)),
        grid=(batch, nt),
        in_specs=in_specs,
        out_specs=(pl.BlockSpec((tq, ch), row_map), pl.BlockSpec((None, HALO, ch), lambda b, i: (b, 0, 0))),
        scratch_shapes=[
            pltpu.VMEM((HALO + tq + SUBLANES, ch), F32),
            pltpu.VMEM((SUBLANES, tq + SUBLANES * ((CONV_W - 1) // SUBLANES), ch), F32),
            pltpu.VMEM((tq, ch), F32),
        ],
        input_output_aliases={len(args) - 1: 0} if aliased else {},
        compiler_params=_cparams(2),
        name="conv_sample" if aliased else "conv_prompt",
    )(*args)


def _mix_kernel(x_ref, a_ref, y_ref, gs_ref, gc_ref, wsb_ref, wcv_ref, wmx_ref, o_ref, *, nn):
    n = pl.program_id(1)

    @pl.when(n == 0)
    def _():
        o_ref[...] = jnp.zeros_like(o_ref)

    o_sb = _dot(a_ref[...].astype(BF16), wsb_ref[...].astype(BF16))
    o_cv = _dot(y_ref[...].astype(BF16), wcv_ref[...].astype(BF16))
    mixed = _sigmoid(gs_ref[...]) * o_sb + _sigmoid(gc_ref[...]) * o_cv
    o_ref[...] += _dot(mixed.astype(BF16), wmx_ref[...].astype(BF16))

    @pl.when(n == nn - 1)
    def _():
        o_ref[...] = x_ref[...] + o_ref[...]


def _mix(x, attn, yact, proj, w_sb_o, w_conv_o, w_mix_o, *, layer, tm, tn, gsb_col, gconv_col):
    m, d = x.shape
    d_sb = attn.shape[1]
    d_conv = yact.shape[1]
    nn = d // tn
    kern = functools.partial(_mix_kernel, nn=nn)
    return pl.pallas_call(
        kern,
        out_shape=jax.ShapeDtypeStruct((m, d), F32),
        grid=(m // tm, nn),
        in_specs=[
            pl.BlockSpec((tm, d), lambda i, n: (i, 0)),
            pl.BlockSpec((tm, d_sb), lambda i, n: (i, 0)),
            pl.BlockSpec((tm, d_conv), lambda i, n: (i, 0)),
            pl.BlockSpec((tm, tn), lambda i, n: (i, gsb_col // tn + n)),
            pl.BlockSpec((tm, tn), lambda i, n: (i, gconv_col // tn + n)),
            pl.BlockSpec((None, d_sb, tn), lambda i, n: (layer, 0, n)),
            pl.BlockSpec((None, d_conv, tn), lambda i, n: (layer, 0, n)),
            pl.BlockSpec((None, tn, d), lambda i, n: (layer, n, 0)),
        ],
        out_specs=pl.BlockSpec((tm, d), lambda i, n: (i, 0)),
        compiler_params=_cparams(2),
        name="mix",
    )(x, attn, yact, proj, proj, w_sb_o, w_conv_o, w_mix_o)


def _cross_kernel(x_ref, g_ref, wq_ref, mk_ref, mv_ref, wo_ref, o_ref, *, heads, scale):
    x = x_ref[...]
    u = _rms(x, g_ref[...]).astype(BF16)
    q = _dot(u, wq_ref[...].astype(BF16))
    outs = []
    for h in range(heads):
        sl = slice(h * HEAD_DIM, (h + 1) * HEAD_DIM)
        s = _dot_nt(q[:, sl].astype(BF16), mk_ref[:, sl].astype(BF16)) * scale
        p = jnp.exp(s - jnp.max(s, axis=-1, keepdims=True))
        o = _dot(p.astype(BF16), mv_ref[:, sl].astype(BF16)) / jnp.sum(p, axis=-1, keepdims=True)
        outs.append(o.astype(BF16))
    o_ref[...] = x + _dot(jnp.concatenate(outs, axis=1), wo_ref[...].astype(BF16))


def _cross(x, g, w_cq, mem_k, mem_v, w_co, *, layer, batch, seq, row0, tm, heads):
    m, d = x.shape
    d_mem = w_cq.shape[2]
    mem_len = mem_k.shape[0] // batch
    nt = seq // tm
    rb0 = row0 // tm
    kern = functools.partial(_cross_kernel, heads=heads, scale=HEAD_DIM ** -0.5)
    return pl.pallas_call(
        kern,
        out_shape=jax.ShapeDtypeStruct((m, d), F32),
        grid=(batch, nt),
        in_specs=[
            pl.BlockSpec((tm, d), lambda b, i: (rb0 + b * nt + i, 0)),
            pl.BlockSpec((None, 1, d), lambda b, i: (layer, 0, 0)),
            pl.BlockSpec((None, d, d_mem), lambda b, i: (layer, 0, 0)),
            pl.BlockSpec((mem_len, d_mem), lambda b, i: (b, 0)),
            pl.BlockSpec((mem_len, d_mem), lambda b, i: (b, 0)),
            pl.BlockSpec((None, d_mem, d), lambda b, i: (layer, 0, 0)),
        ],
        out_specs=pl.BlockSpec((tm, d), lambda b, i: (rb0 + b * nt + i, 0)),
        input_output_aliases={0: 0},
        compiler_params=_cparams(2),
        name="cross",
    )(x, _layer_vec(g), w_cq, mem_k, mem_v, w_co)


def _largest_tile(n, cap, mult):
    best = None
    for t in range(mult, cap + 1, mult):
        if n % t == 0:
            best = t
    assert best is not None, (n, cap, mult)
    return best


def kernel(x_prompt, x_sample, mem_prompt, cache_k, cache_v, state_conv, cache_mem_k, cache_mem_v, page_table, norm_ffa, ffa_w_gate, ffa_w_up, ffa_w_down, norm_mix, w_in, sb_bias, conv_w_dw, conv_b_dw, conv_ln_g, conv_ln_b, w_conv_o, w_sb_o, w_mix_o, norm_cross, norm_mem, w_cq, w_ck, w_cv, w_co, norm_ffb, ffb_w_gate, ffb_w_up, ffb_w_down, norm_final):
    batch, seq, d = x_prompt.shape
    dec_b, dec_seq, _ = x_sample.shape
    depth = w_in.shape[0]
    heads = cache_k.shape[3]
    d_sb = heads * HEAD_DIM
    d_conv = conv_w_dw.shape[2]
    mem_len = mem_prompt.shape[1]
    mem_heads = cache_mem_k.shape[3]
    d_mem = mem_heads * HEAD_DIM
    n_prompt = batch * seq
    n_tok = n_prompt + dec_b * dec_seq
    val_col, gsb_col, gconv_col = 3 * d_sb, 3 * d_sb + 2 * d_conv, 3 * d_sb + 2 * d_conv + d
    assert d_conv == d_sb and val_col % d_conv == 0

    tm = _largest_tile(n_tok, 768, 16)
    tm_big = _largest_tile(n_tok, 1536, 16)
    tn_in =_largest_tile(math.gcd(w_in.shape[2], d_sb), 512, 128)
    tq_attn = min(seq, 256)
    tq_conv = min(seq, 256)
    tm_cross = min(seq, 512)
    q_scale = HEAD_DIM ** -0.5 * LOG2E
    bias2 = sb_bias * LOG2E

    x = jnp.concatenate([x_prompt.reshape(n_prompt, d), x_sample.reshape(dec_b * dec_seq, d)], axis=0)
    mem = mem_prompt.reshape(batch * mem_len, d)
    ck = cache_k.reshape(depth, cache_k.shape[1], PAGE * heads, HEAD_DIM)
    cv = cache_v.reshape(depth, cache_v.shape[1], PAGE * heads, HEAD_DIM)
    zero_buf = jnp.zeros((batch, HALO, d_conv), F32)
    state_pad = jnp.pad(state_conv, ((0, 0), (0, 0), (HALO - (CONV_W - 1), 0), (0, 0)))

    outs = {k: [] for k in ("kp", "vp", "cp", "mk", "mv", "ks", "vs", "cs")}
    for l in range(depth):
        x = _ffn(x, norm_ffa, ffa_w_gate, ffa_w_up, ffa_w_down, norm_final, layer=l, tm=tm_big, tf=512, final_norm=False)
        proj, qkv = _in_proj(x, norm_mix, w_in, layer=l, tm=tm_big, tn=tn_in, d_sb=d_sb, q_scale=q_scale)

        attn = _sb_prompt(qkv, bias2[l], n_tok=n_tok, batch=batch, seq=seq, heads=heads, tq=tq_attn)
        attn = _sb_sample(attn, proj, ck, cv, page_table, bias2[l], layer=l, row0=n_prompt, heads=heads, lq=dec_seq,
                          q_scale=q_scale)

        yact, buf_p = _conv(None, proj, zero_buf, conv_w_dw, conv_b_dw, conv_ln_g, conv_ln_b, layer=l,
                            batch=batch, seq=seq, row0=0, tq=tq_conv, val_blk=val_col // d_conv)
        yact, buf_s = _conv(yact, proj, state_pad[l], conv_w_dw, conv_b_dw, conv_ln_g, conv_ln_b, layer=l,
                            batch=dec_b, seq=dec_seq, row0=n_prompt, tq=dec_seq, val_blk=val_col // d_conv)

        x = _mix(x, attn, yact, proj, w_sb_o, w_conv_o, w_mix_o, layer=l, tm=tm, tn=256,
                 gsb_col=gsb_col, gconv_col=gconv_col)

        mk = _norm_matmul(mem, norm_mem, w_ck, layer=l, tm=batch * mem_len, tn=d_mem, name="mem_k")
        mv = _norm_matmul(mem, norm_mem, w_cv, layer=l, tm=batch * mem_len, tn=d_mem, name="mem_v")
        x = _cross(x, norm_cross, w_cq, mk, mv, w_co, layer=l, batch=batch, seq=seq, row0=0, tm=tm_cross,
                   heads=mem_heads)
        x = _cross(x, norm_cross, w_cq, cache_mem_k[l].reshape(dec_b * mem_len, d_mem),
                   cache_mem_v[l].reshape(dec_b * mem_len, d_mem), w_co, layer=l,
                   batch=dec_b, seq=dec_seq, row0=n_prompt, tm=dec_seq, heads=mem_heads)

        x = _ffn(x, norm_ffb, ffb_w_gate, ffb_w_up, ffb_w_down, norm_final, layer=l, tm=tm_big, tf=512,
                 final_norm=(l == depth - 1))

        outs["kp"].append(proj[:n_prompt, d_sb:2 * d_sb].reshape(batch, seq, heads, HEAD_DIM))
        outs["vp"].append(proj[:n_prompt, 2 * d_sb:3 * d_sb].reshape(batch, seq, heads, HEAD_DIM))
        outs["cp"].append(buf_p[:, HALO - (CONV_W - 1):])
        outs["mk"].append(mk.reshape(batch, mem_len, mem_heads, HEAD_DIM))
        outs["mv"].append(mv.reshape(batch, mem_len, mem_heads, HEAD_DIM))
        outs["ks"].append(proj[n_prompt:, d_sb:2 * d_sb].reshape(dec_b, dec_seq, heads, HEAD_DIM))
        outs["vs"].append(proj[n_prompt:, 2 * d_sb:3 * d_sb].reshape(dec_b, dec_seq, heads, HEAD_DIM))
        outs["cs"].append(buf_s[:, HALO - (CONV_W - 1):])

    y_prompt = x[:n_prompt].reshape(batch, seq, d)
    y_sample = x[n_prompt:].reshape(dec_b, dec_seq, d)
    return (y_prompt, y_sample, jnp.stack(outs["kp"]), jnp.stack(outs["vp"]), jnp.stack(outs["cp"]),
            jnp.stack(outs["mk"]), jnp.stack(outs["mv"]), jnp.stack(outs["ks"]), jnp.stack(outs["vs"]),
            jnp.stack(outs["cs"]))
```

```python
import functools
import math

import jax
import jax.numpy as jnp
from jax import lax
from jax.experimental import pallas as pl
from jax.experimental.pallas import tpu as pltpu

EPS = 1e-6
PAGE = 128
HEAD_DIM = 128
SUBLANES = 8
CONV_W = 31
HALO = 32
VMEM_LIMIT = 60 * 1024 * 1024
LOG2E = math.log2(math.e)

BF16 = jnp.bfloat16
F32 = jnp.float32


def _cparams(n_axes):
    return pltpu.CompilerParams(dimension_semantics=("arbitrary",) * n_axes, vmem_limit_bytes=VMEM_LIMIT)


def _rms(x, g):
    return x * lax.rsqrt(jnp.mean(x * x, axis=-1, keepdims=True) + EPS) * g


def _dot(a, b):
    return jnp.dot(a, b, preferred_element_type=F32)


def _dot_nt(a, b):
    return lax.dot_general(a, b, (((1,), (1,)), ((), ())), preferred_element_type=F32)


def _sigmoid(x):
    return 1.0 / (1.0 + jnp.exp(-x))


def _layer_vec(g):
    return g.reshape(g.shape[0], 1, g.shape[1])


def _ffn_kernel(*refs, nf, tf, tail, final_norm, n_tiles, n_main, split_in, split_out):
    refs = list(refs)
    x_ref = refs.pop(0)
    xs_ref = refs.pop(0) if split_in else None
    g_ref, wg_ref, wu_ref, wd_ref, gf_ref, o_ref = refs[:6]
    os_ref = refs[6] if split_out else None
    u_ref = refs[-1]
    i = pl.program_id(0)
    f = pl.program_id(1)

    @pl.when(f == 0)
    def _():
        if split_in:
            @pl.when(i < n_tiles - 1)
            def _():
                o_ref[...] = x_ref[...]

            @pl.when(i == n_tiles - 1)
            def _():
                o_ref[0:n_main, :] = x_ref[0:n_main, :]
                o_ref[n_main:, :] = xs_ref[...]
        else:
            o_ref[...] = x_ref[...]
        u_ref[...] = _rms(o_ref[...], g_ref[...]).astype(BF16)

    def step(width):
        u = u_ref[...]
        gate = _dot(u, wg_ref[:, :width].astype(BF16))
        up = _dot(u, wu_ref[:, :width].astype(BF16))
        h = (0.5 * (gate * _sigmoid(gate) * up)).astype(BF16)
        o_ref[...] += _dot(h, wd_ref[:width, :].astype(BF16))

    if tail == tf:
        step(tf)
    else:
        @pl.when(f < nf - 1)
        def _():
            step(tf)

        @pl.when(f == nf - 1)
        def _():
            step(tail)

    if final_norm or split_out:
        @pl.when(f == nf - 1)
        def _():
            if final_norm:
                o_ref[...] = _rms(o_ref[...], gf_ref[...])
            if split_out:
                @pl.when(i == n_tiles - 1)
                def _():
                    os_ref[...] = o_ref[n_main:, :]


def _ffn(x, g, wg, wu, wd, gfin, *, layer, tm, tf, final_norm, x_sample=None, n_sample=0):
    split_in, split_out = x_sample is not None, n_sample > 0
    d = x.shape[1]
    m = x.shape[0] + (x_sample.shape[0] if split_in else 0)
    n_extra = x_sample.shape[0] if split_in else n_sample
    n_tiles = m // tm
    n_main = tm - n_extra
    assert n_main > 0 and n_main % SUBLANES == 0
    dff = wg.shape[2]
    nf = pl.cdiv(dff, tf)
    tail = dff - (nf - 1) * tf
    kern = functools.partial(_ffn_kernel, nf=nf, tf=tf, tail=tail, final_norm=final_norm, n_tiles=n_tiles,
                             n_main=n_main, split_in=split_in, split_out=split_out)
    rows = lambda: pl.BlockSpec((tm, d), lambda i, f: (i, 0))
    extra = lambda: pl.BlockSpec((n_extra, d), lambda i, f: (0, 0))
    in_specs = [rows()] + ([extra()] if split_in else []) + [
        pl.BlockSpec((None, 1, d), lambda i, f: (layer, 0, 0)),
        pl.BlockSpec((None, d, tf), lambda i, f: (layer, 0, f)),
        pl.BlockSpec((None, d, tf), lambda i, f: (layer, 0, f)),
        pl.BlockSpec((None, tf, d), lambda i, f: (layer, f, 0)),
        pl.BlockSpec((1, d), lambda i, f: (0, 0)),
    ]
    args = [x] + ([x_sample] if split_in else []) + [_layer_vec(g), wg, wu, wd, gfin.reshape(1, d)]
    if split_out:
        out_shape = (jax.ShapeDtypeStruct((m - n_extra, d), F32), jax.ShapeDtypeStruct((n_extra, d), F32))
        out_specs = (rows(), extra())
    else:
        out_shape = jax.ShapeDtypeStruct((m, d), F32)
        out_specs = rows()
    return pl.pallas_call(
        kern,
        out_shape=out_shape,
        grid=(n_tiles, nf),
        in_specs=in_specs,
        out_specs=out_specs,
        scratch_shapes=[pltpu.VMEM((tm, d), BF16)],
        compiler_params=_cparams(2),
        name="ffn",
    )(*args)


def _norm_matmul_kernel(x_ref, g_ref, w_ref, o_ref, u_ref):
    @pl.when(pl.program_id(1) == 0)
    def _():
        u_ref[...] = _rms(x_ref[...], g_ref[...]).astype(BF16)

    o_ref[...] = _dot(u_ref[...], w_ref[...].astype(BF16))


def _norm_matmul(x, g, w, *, layer, tm, tn, name):
    m, d = x.shape
    n = w.shape[2]
    return pl.pallas_call(
        _norm_matmul_kernel,
        out_shape=jax.ShapeDtypeStruct((m, n), F32),
        grid=(m // tm, n // tn),
        in_specs=[
            pl.BlockSpec((tm, d), lambda i, j: (i, 0)),
            pl.BlockSpec((None, 1, d), lambda i, j: (layer, 0, 0)),
            pl.BlockSpec((None, d, tn), lambda i, j: (layer, 0, j)),
        ],
        out_specs=pl.BlockSpec((tm, tn), lambda i, j: (i, j)),
        scratch_shapes=[pltpu.VMEM((tm, d), BF16)],
        compiler_params=_cparams(2),
        name=name,
    )(x, _layer_vec(g), w)


def _in_proj_kernel(x_ref, g_ref, w_ref, o_ref, qkv_ref, u_ref, *, q_blocks, qkv_blocks, q_scale):
    j = pl.program_id(1)

    @pl.when(j == 0)
    def _():
        u_ref[...] = _rms(x_ref[...], g_ref[...]).astype(BF16)

    acc = _dot(u_ref[...], w_ref[...].astype(BF16))
    o_ref[...] = acc

    @pl.when(j < q_blocks)
    def _():
        qkv_ref[...] = (acc * q_scale).astype(BF16)

    @pl.when((j >= q_blocks) & (j < qkv_blocks))
    def _():
        qkv_ref[...] = acc.astype(BF16)


def _in_proj(x, g, w, *, layer, tm, tn, d_sb, q_scale):
    m, d = x.shape
    n = w.shape[2]
    q_blocks, qkv_blocks = d_sb // tn, 3 * d_sb // tn
    kern = functools.partial(_in_proj_kernel, q_blocks=q_blocks, qkv_blocks=qkv_blocks, q_scale=q_scale)
    return pl.pallas_call(
        kern,
        out_shape=(jax.ShapeDtypeStruct((m, n), F32), jax.ShapeDtypeStruct((m, 3 * d_sb), BF16)),
        grid=(m // tm, n // tn),
        in_specs=[
            pl.BlockSpec((tm, d), lambda i, j: (i, 0)),
            pl.BlockSpec((None, 1, d), lambda i, j: (layer, 0, 0)),
            pl.BlockSpec((None, d, tn), lambda i, j: (layer, 0, j)),
        ],
        out_specs=(
            pl.BlockSpec((tm, tn), lambda i, j: (i, j)),
            pl.BlockSpec((tm, tn), lambda i, j: (i, jnp.minimum(j, qkv_blocks - 1))),
        ),
        scratch_shapes=[pltpu.VMEM((tm, d), BF16)],
        compiler_params=_cparams(2),
        name="in_proj",
    )(x, _layer_vec(g), w)


MASKED = -1e30


def _sb_sums(z, tri):
    neg_abs = lax.bitcast_convert_type(lax.bitcast_convert_type(z, jnp.uint32) | jnp.uint32(0x80000000), F32)
    soft = jnp.maximum(z, 0.0) + jnp.log(1.0 + jnp.exp2(neg_abs)) * LOG2E
    parts = []
    for _ in range(tri.shape[0] // z.shape[1]):
        parts.append(soft.astype(BF16))
        soft = soft - parts[-1].astype(F32)
    return _dot(parts[0] if len(parts) == 1 else jnp.concatenate(parts, axis=1), tri)


def _sb_weights(z, csum, carry):
    return jnp.exp2(z - csum - carry), carry + csum[:, 0:1]


def _tri(keys, parts):
    j = lax.broadcasted_iota(jnp.int32, (parts * keys, keys), 0) & (keys - 1)
    s = lax.broadcasted_iota(jnp.int32, (parts * keys, keys), 1)
    return (j >= s).astype(BF16)


def _sbp_kernel(bias_ref, tile_ref, pair_ref, kind_ref, q_ref, k_ref, v_ref, tri_ref, o_ref,
                z_ref, e_ref, tot_ref, mb_ref, *, tq, n_iter):
    bias = bias_ref[pl.program_id(1)]
    tri = tri_ref[...]
    wide = 2 * tq
    diff = lax.broadcasted_iota(jnp.int32, (tq, wide), 1) - lax.broadcasted_iota(jnp.int32, (tq, wide), 0)
    for kind, bound in enumerate((wide, 0, tq)):
        mb_ref[kind] = jnp.where(diff < bound, bias, MASKED)

    def where(t):
        i, p = tile_ref[t], pair_ref[t]
        return i, p, pl.ds(pl.multiple_of(i * tq, tq), tq), pl.ds(pl.multiple_of(p * wide, wide), wide)

    def logits(t):
        _, _, q_rows, k_rows = where(t)
        z_ref[...] = _dot_nt(q_ref[q_rows, :], k_ref[k_rows, :]) + mb_ref[kind_ref[t]]

    def sums():
        for s in range(2):
            z = z_ref[:, s * tq:(s + 1) * tq]
            csum = _sb_sums(z, tri)
            e_ref[:, s * tq:(s + 1) * tq] = z - csum
            tot_ref[s] = csum[:, 0:1]

    def apply(t, carry, acc):
        i, p, q_rows, k_rows = where(t)
        fresh = p == i // 2
        carry = jnp.where(fresh, 0.0, carry)
        acc = jnp.where(fresh, 0.0, acc)
        w1 = jnp.exp2(e_ref[:, tq:] - carry)
        carry = carry + tot_ref[1]
        w0 = jnp.exp2(e_ref[:, :tq] - carry)
        carry = carry + tot_ref[0]
        acc = acc + _dot(jnp.concatenate([w0, w1], axis=1).astype(BF16), v_ref[k_rows, :])
        o_ref[q_rows, :] = acc
        return carry, acc

    state = (jnp.zeros((tq, 1), F32), jnp.zeros((tq, HEAD_DIM), F32))
    logits(0)
    sums()
    logits(1)

    def body(t, st):
        st = apply(t - 2, *st)
        sums()
        logits(t)
        return st

    state = lax.fori_loop(2, n_iter, body, state, unroll=7 if (n_iter - 2) % 7 == 0 else 2)
    state = apply(n_iter - 2, *state)
    sums()
    apply(n_iter - 1, *state)


def _sb_prompt(qkv, bias2, *, n_tok, batch, seq, heads, tq):
    nq = seq // tq
    assert nq % 2 == 0, "key blocks are taken in pairs"
    steps = [(i, p) for i in range(nq) for p in range(i // 2, -1, -1)]
    tiles = jnp.asarray([i for i, _ in steps], jnp.int32)
    pairs = jnp.asarray([p for _, p in steps], jnp.int32)
    kinds = jnp.asarray([0 if p < i // 2 else 1 + i % 2 for i, p in steps], jnp.int32)
    kern = functools.partial(_sbp_kernel, tq=tq, n_iter=len(steps))
    grid_spec = pltpu.PrefetchScalarGridSpec(
        num_scalar_prefetch=4,
        grid=(batch, heads),
        in_specs=[
            pl.BlockSpec((seq, HEAD_DIM), lambda b, h, *_: (b, h)),
            pl.BlockSpec((seq, HEAD_DIM), lambda b, h, *_: (b, heads + h)),
            pl.BlockSpec((seq, HEAD_DIM), lambda b, h, *_: (b, 2 * heads + h)),
            pl.BlockSpec((tq, tq), lambda b, h, *_: (0, 0)),
        ],
        out_specs=pl.BlockSpec((seq, HEAD_DIM), lambda b, h, *_: (b, h)),
        scratch_shapes=[
            pltpu.VMEM((tq, 2 * tq), F32),
            pltpu.VMEM((tq, 2 * tq), F32),
            pltpu.VMEM((2, tq, 1), F32),
            pltpu.VMEM((3, tq, 2 * tq), F32),
        ],
    )
    return pl.pallas_call(
        kern,
        out_shape=jax.ShapeDtypeStruct((n_tok, heads * HEAD_DIM), F32),
        grid_spec=grid_spec,
        compiler_params=_cparams(2),
        name="sb_prompt",
    )(bias2, tiles, pairs, kinds, qkv, qkv, qkv, _tri(tq, 1))


def _sbs_kernel(pt_ref, q_ref, kn_ref, vn_ref, *rest, heads, lq, pages, n_steps, q_scale):
    kp_refs = rest[:pages]
    vp_refs = rest[pages:2 * pages]
    bias_ref, tri_ref, _, o_ref, carry_ref, acc_ref = rest[2 * pages:]
    j = pl.program_id(1)
    pad = jnp.zeros((lq, HEAD_DIM), F32)

    def q_head(h):
        qh = q_ref[:, h * HEAD_DIM:(h + 1) * HEAD_DIM] * q_scale
        return jnp.concatenate([qh, pad], axis=0).astype(BF16)

    def attend(k_heads, v_heads, mask):
        rows = heads * lq
        z = jnp.concatenate(
            [jnp.concatenate([_dot_nt(q_head(h), k_head(h))[0:lq] for h in range(heads)], axis=0) + bias_ref[...]
             for k_head in k_heads], axis=0)
        if mask is not None:
            z = jnp.where(mask, z, MASKED)
        csum = _sb_sums(z, tri_ref[...])
        carry = carry_ref[...]
        acc = [acc_ref[h * lq:(h + 1) * lq, :] for h in range(heads)]
        for s, v_head in enumerate(v_heads):
            w, carry = _sb_weights(z[s * rows:(s + 1) * rows], csum[s * rows:(s + 1) * rows], carry)
            for h in range(heads):
                wh = jnp.concatenate([w[h * lq:(h + 1) * lq], pad], axis=0).astype(BF16)
                acc[h] = acc[h] + _dot(wh, v_head(h))[0:lq]
        carry_ref[...] = carry
        acc_ref[...] = jnp.concatenate(acc, axis=0)

    @pl.when(j == 0)
    def _():
        carry_ref[...] = jnp.zeros_like(carry_ref)
        acc_ref[...] = jnp.zeros_like(acc_ref)
        fill = jnp.zeros((PAGE - lq, HEAD_DIM), F32)
        new = lambda ref: lambda h: jnp.concatenate(
            [ref[:, h * HEAD_DIM:(h + 1) * HEAD_DIM], fill], axis=0).astype(BF16)
        key = lax.broadcasted_iota(jnp.int32, (heads * lq, PAGE), 1)
        t = lax.broadcasted_iota(jnp.int32, (heads * lq, PAGE), 0) & (lq - 1)
        attend([new(kn_ref)], [new(vn_ref)], key < t)

    @pl.when(j > 0)
    def _():
        cached = lambda ref: lambda h: ref[pl.ds(h, PAGE, stride=heads), :].astype(BF16)
        attend([cached(r) for r in kp_refs], [cached(r) for r in vp_refs], None)

    @pl.when(j == n_steps - 1)
    def _():
        for h in range(heads):
            o_ref[:, h * HEAD_DIM:(h + 1) * HEAD_DIM] = acc_ref[h * lq:(h + 1) * lq, :]


def _sb_sample(attn, proj, cache_k, cache_v, page_table, bias2, *, layer, row0, heads, lq, q_scale):
    dec_b, n_pages = page_table.shape
    assert lq & (lq - 1) == 0
    width = heads * HEAD_DIM
    rows = heads * lq
    pages = max(p for p in (8, 4, 2, 1) if n_pages % p == 0)
    n_steps = n_pages // pages + 1
    rb0 = row0 // lq
    kern = functools.partial(_sbs_kernel, heads=heads, lq=lq, pages=pages, n_steps=n_steps, q_scale=q_scale)

    def page(s):
        def index(b, j, pt):
            pos = n_pages - 1 - (jnp.maximum(j, 1) - 1) * pages - s
            return (layer, pt[b, pos], 0, 0)
        return pl.BlockSpec((None, None, PAGE * heads, HEAD_DIM), index)

    row_spec = lambda col: pl.BlockSpec((lq, width), lambda b, j, pt: (rb0 + b, col))
    grid_spec = pltpu.PrefetchScalarGridSpec(
        num_scalar_prefetch=1,
        grid=(dec_b, n_steps),
        in_specs=[row_spec(0), row_spec(1), row_spec(2)]
        + [page(s) for s in range(pages)] + [page(s) for s in range(pages)]
        + [
            pl.BlockSpec((rows, PAGE), lambda b, j, pt: (0, 0)),
            pl.BlockSpec((2 * PAGE, PAGE), lambda b, j, pt: (0, 0)),
            pl.BlockSpec(memory_space=pl.ANY),
        ],
        out_specs=row_spec(0),
        scratch_shapes=[pltpu.VMEM((rows, 1), F32), pltpu.VMEM((rows, HEAD_DIM), F32)],
    )
    bias_rows = jnp.broadcast_to(jnp.repeat(bias2, lq)[:, None], (rows, PAGE))
    n_in = 1 + 3 + 2 * pages + 3
    return pl.pallas_call(
        kern,
        out_shape=jax.ShapeDtypeStruct(attn.shape, attn.dtype),
        grid_spec=grid_spec,
        input_output_aliases={n_in - 1: 0},
        compiler_params=_cparams(2),
        name="sb_sample",
    )(page_table, proj, proj, proj, *([cache_k] * pages), *([cache_v] * pages), bias_rows, _tri(PAGE, 2), attn)


def _conv_kernel(cv_ref, cg_ref, init_ref, w_ref, b_ref, lg_ref, lb_ref, *rest, tq, nt, rc, cc):
    y_ref, buf_ref, ext_ref, rot_ref, acc_ref = rest[-5:]
    i = pl.program_id(1)
    ch = cv_ref.shape[1]

    @pl.when(i == 0)
    def _():
        ext_ref[0:HALO, :] = init_ref[...]
        ext_ref[HALO + tq:HALO + tq + SUBLANES, :] = jnp.zeros((SUBLANES, ch), F32)

    ext_ref[HALO:HALO + tq, :] = cv_ref[...] * _sigmoid(cg_ref[...])
    first = HALO - (CONV_W - 1)
    span = rot_ref.shape[1]
    for s in range(SUBLANES):
        rot_ref[s] = ext_ref[first + s:first + s + span, :]
    for r in range(0, tq, rc):
        for c in range(0, ch, cc):
            acc = jnp.broadcast_to(b_ref[:, c:c + cc], (rc, cc))
            for j in range(CONV_W):
                a, s = divmod(j, SUBLANES)
                row = r + SUBLANES * a
                acc = acc + rot_ref[s, row:row + rc, c:c + cc] * w_ref[j:j + 1, c:c + cc]
            acc_ref[r:r + rc, c:c + cc] = acc
    y = acc_ref[...]
    mu = jnp.mean(y, axis=-1, keepdims=True)
    yc = y - mu
    y = yc * lax.rsqrt(jnp.mean(yc * yc, axis=-1, keepdims=True) + EPS) * lg_ref[...] + lb_ref[...]
    y_ref[...] = (y * _sigmoid(y)).astype(y_ref.dtype)

    @pl.when(i == nt - 1)
    def _():
        buf_ref[...] = ext_ref[tq:tq + HALO, :]

    if nt > 1:
        @pl.when(i < nt - 1)
        def _():
            ext_ref[0:HALO, :] = ext_ref[tq:tq + HALO, :]


def _conv(yact, proj, init, w_dw, b_dw, ln_g, ln_b, *, layer, batch, seq, row0, tq, val_blk):
    ch = w_dw.shape[2]
    aliased = yact is not None
    nt = seq // tq
    rb0 = row0 // tq
    rc = min(tq, 64)
    kern = functools.partial(_conv_kernel, tq=tq, nt=nt, rc=rc, cc=256)
    row_map = lambda b, i: (rb0 + b * nt + i, 0)
    vec = pl.BlockSpec((None, 1, ch), lambda b, i: (layer, 0, 0))
    in_specs = [
        pl.BlockSpec((tq, ch), lambda b, i: (rb0 + b * nt + i, val_blk)),
        pl.BlockSpec((tq, ch), lambda b, i: (rb0 + b * nt + i, val_blk + 1)),
        pl.BlockSpec((None, HALO, ch), lambda b, i: (b, 0, 0)),
        pl.BlockSpec((None, CONV_W, ch), lambda b, i: (layer, 0, 0)),
        vec, vec, vec,
    ]
    args = [proj, proj, init, w_dw, _layer_vec(b_dw), _layer_vec(ln_g), _layer_vec(ln_b)]
    if aliased:
        in_specs.append(pl.BlockSpec(memory_space=pl.ANY))
        args.append(yact)
    return pl.pallas_call(
        kern,
        out_shape=(jax.ShapeDtypeStruct((proj.shape[0], ch), F32), jax.ShapeDtypeStruct((batch, HALO, ch), F32)),
        grid=(batch, nt),
        in_specs=in_specs,
        out_specs=(pl.BlockSpec((tq, ch), row_map), pl.BlockSpec((None, HALO, ch), lambda b, i: (b, 0, 0))),
        scratch_shapes=[
            pltpu.VMEM((HALO + tq + SUBLANES, ch), F32),
            pltpu.VMEM((SUBLANES, tq + SUBLANES * ((CONV_W - 1) // SUBLANES), ch), F32),
            pltpu.VMEM((tq, ch), F32),
        ],
        input_output_aliases={len(args) - 1: 0} if aliased else {},
        compiler_params=_cparams(2),
        name="conv_sample" if aliased else "conv_prompt",
    )(*args)


def _mix_kernel(x_ref, a_ref, y_ref, gs_ref, gc_ref, wsb_ref, wcv_ref, wmx_ref, o_ref, *, nn):
    n = pl.program_id(1)

    @pl.when(n == 0)
    def _():
        o_ref[...] = jnp.zeros_like(o_ref)

    o_sb = _dot(a_ref[...].astype(BF16), wsb_ref[...].astype(BF16))
    o_cv = _dot(y_ref[...].astype(BF16), wcv_ref[...].astype(BF16))
    mixed = _sigmoid(gs_ref[...]) * o_sb + _sigmoid(gc_ref[...]) * o_cv
    o_ref[...] += _dot(mixed.astype(BF16), wmx_ref[...].astype(BF16))

    @pl.when(n == nn - 1)
    def _():
        o_ref[...] = x_ref[...] + o_ref[...]


def _mix(x, attn, yact, proj, w_sb_o, w_conv_o, w_mix_o, *, layer, tm, tn, gsb_col, gconv_col):
    m, d = x.shape
    d_sb = attn.shape[1]
    d_conv = yact.shape[1]
    nn = d // tn
    kern = functools.partial(_mix_kernel, nn=nn)
    return pl.pallas_call(
        kern,
        out_shape=jax.ShapeDtypeStruct((m, d), F32),
        grid=(m // tm, nn),
        in_specs=[
            pl.BlockSpec((tm, d), lambda i, n: (i, 0)),
            pl.BlockSpec((tm, d_sb), lambda i, n: (i, 0)),
            pl.BlockSpec((tm, d_conv), lambda i, n: (i, 0)),
            pl.BlockSpec((tm, tn), lambda i, n: (i, gsb_col // tn + n)),
            pl.BlockSpec((tm, tn), lambda i, n: (i, gconv_col // tn + n)),
            pl.BlockSpec((None, d_sb, tn), lambda i, n: (layer, 0, n)),
            pl.BlockSpec((None, d_conv, tn), lambda i, n: (layer, 0, n)),
            pl.BlockSpec((None, tn, d), lambda i, n: (layer, n, 0)),
        ],
        out_specs=pl.BlockSpec((tm, d), lambda i, n: (i, 0)),
        compiler_params=_cparams(2),
        name="mix",
    )(x, attn, yact, proj, proj, w_sb_o, w_conv_o, w_mix_o)


def _cross_kernel(x_ref, g_ref, wq_ref, mk_ref, mv_ref, wo_ref, o_ref, *, heads, scale):
    x = x_ref[...]
    u = _rms(x, g_ref[...]).astype(BF16)
    q = _dot(u, wq_ref[...].astype(BF16))
    outs = []
    for h in range(heads):
        sl = slice(h * HEAD_DIM, (h + 1) * HEAD_DIM)
        s = _dot_nt(q[:, sl].astype(BF16), mk_ref[:, sl].astype(BF16)) * scale
        p = jnp.exp(s - jnp.max(s, axis=-1, keepdims=True))
        o = _dot(p.astype(BF16), mv_ref[:, sl].astype(BF16)) / jnp.sum(p, axis=-1, keepdims=True)
        outs.append(o.astype(BF16))
    o_ref[...] = x + _dot(jnp.concatenate(outs, axis=1), wo_ref[...].astype(BF16))


def _cross(x, g, w_cq, mem_k, mem_v, w_co, *, layer, batch, seq, row0, tm, heads):
    m, d = x.shape
    d_mem = w_cq.shape[2]
    mem_len = mem_k.shape[0] // batch
    nt = seq // tm
    rb0 = row0 // tm
    kern = functools.partial(_cross_kernel, heads=heads, scale=HEAD_DIM ** -0.5)
    return pl.pallas_call(
        kern,
        out_shape=jax.ShapeDtypeStruct((m, d), F32),
        grid=(batch, nt),
        in_specs=[
            pl.BlockSpec((tm, d), lambda b, i: (rb0 + b * nt + i, 0)),
            pl.BlockSpec((None, 1, d), lambda b, i: (layer, 0, 0)),
            pl.BlockSpec((None, d, d_mem), lambda b, i: (layer, 0, 0)),
            pl.BlockSpec((mem_len, d_mem), lambda b, i: (b, 0)),
            pl.BlockSpec((mem_len, d_mem), lambda b, i: (b, 0)),
            pl.BlockSpec((None, d_mem, d), lambda b, i: (layer, 0, 0)),
        ],
        out_specs=pl.BlockSpec((tm, d), lambda b, i: (rb0 + b * nt + i, 0)),
        input_output_aliases={0: 0},
        compiler_params=_cparams(2),
        name="cross",
    )(x, _layer_vec(g), w_cq, mem_k, mem_v, w_co)


def _largest_tile(n, cap, mult):
    best = None
    for t in range(mult, cap + 1, mult):
        if n % t == 0:
            best = t
    assert best is not None, (n, cap, mult)
    return best


def kernel(x_prompt, x_sample, mem_prompt, cache_k, cache_v, state_conv, cache_mem_k, cache_mem_v, page_table, norm_ffa, ffa_w_gate, ffa_w_up, ffa_w_down, norm_mix, w_in, sb_bias, conv_w_dw, conv_b_dw, conv_ln_g, conv_ln_b, w_conv_o, w_sb_o, w_mix_o, norm_cross, norm_mem, w_cq, w_ck, w_cv, w_co, norm_ffb, ffb_w_gate, ffb_w_up, ffb_w_down, norm_final):
    batch, seq, d = x_prompt.shape
    dec_b, dec_seq, _ = x_sample.shape
    depth = w_in.shape[0]
    heads = cache_k.shape[3]
    d_sb = heads * HEAD_DIM
    d_conv = conv_w_dw.shape[2]
    mem_len = mem_prompt.shape[1]
    mem_heads = cache_mem_k.shape[3]
    d_mem = mem_heads * HEAD_DIM
    n_prompt = batch * seq
    n_tok = n_prompt + dec_b * dec_seq
    val_col, gsb_col, gconv_col = 3 * d_sb, 3 * d_sb + 2 * d_conv, 3 * d_sb + 2 * d_conv + d
    assert d_conv == d_sb and val_col % d_conv == 0

    tm = _largest_tile(n_tok, 768, 16)
    tm_big = _largest_tile(n_tok, 1536, 16)
    tn_in =_largest_tile(math.gcd(w_in.shape[2], d_sb), 512, 128)
    tq_attn = min(seq, 256)
    tq_conv = min(seq, 256)
    tm_cross = min(seq, 512)
    q_scale = HEAD_DIM ** -0.5 * LOG2E
    bias2 = sb_bias * LOG2E

    n_sample = dec_b * dec_seq
    x = None
    mem = mem_prompt.reshape(batch * mem_len, d)
    w_sb_o, w_conv_o, w_mix_o = w_sb_o.astype(BF16), w_conv_o.astype(BF16), w_mix_o.astype(BF16)
    ck = cache_k.reshape(depth, cache_k.shape[1], PAGE * heads, HEAD_DIM)
    cv = cache_v.reshape(depth, cache_v.shape[1], PAGE * heads, HEAD_DIM)
    zero_buf = jnp.zeros((batch, HALO, d_conv), F32)
    state_pad = jnp.pad(state_conv, ((0, 0), (0, 0), (HALO - (CONV_W - 1), 0), (0, 0)))

    outs = {k: [] for k in ("kp", "vp", "cp", "mk", "mv", "ks", "vs", "cs")}
    for l in range(depth):
        if l == 0:
            x = _ffn(x_prompt.reshape(n_prompt, d), norm_ffa, ffa_w_gate, ffa_w_up, ffa_w_down, norm_final, layer=l,
                     tm=tm, tf=512, final_norm=False, x_sample=x_sample.reshape(n_sample, d))
        else:
            x = _ffn(x, norm_ffa, ffa_w_gate, ffa_w_up, ffa_w_down, norm_final, layer=l, tm=tm, tf=512,
                     final_norm=False)
        proj, qkv = _in_proj(x, norm_mix, w_in, layer=l, tm=tm_big, tn=tn_in, d_sb=d_sb, q_scale=q_scale)

        attn = _sb_prompt(qkv, bias2[l], n_tok=n_tok, batch=batch, seq=seq, heads=heads, tq=tq_attn)
        attn = _sb_sample(attn, proj, ck, cv, page_table, bias2[l], layer=l, row0=n_prompt, heads=heads, lq=dec_seq,
                          q_scale=q_scale)

        yact, buf_p = _conv(None, proj, zero_buf, conv_w_dw, conv_b_dw, conv_ln_g, conv_ln_b, layer=l,
                            batch=batch, seq=seq, row0=0, tq=tq_conv, val_blk=val_col // d_conv)
        yact, buf_s = _conv(yact, proj, state_pad[l], conv_w_dw, conv_b_dw, conv_ln_g, conv_ln_b, layer=l,
                            batch=dec_b, seq=dec_seq, row0=n_prompt, tq=dec_seq, val_blk=val_col // d_conv)

        x = _mix(x, attn, yact, proj, w_sb_o, w_conv_o, w_mix_o, layer=l, tm=tm, tn=256,
                 gsb_col=gsb_col, gconv_col=gconv_col)

        mk = _norm_matmul(mem, norm_mem, w_ck, layer=l, tm=batch * mem_len, tn=d_mem, name="mem_k")
        mv = _norm_matmul(mem, norm_mem, w_cv, layer=l, tm=batch * mem_len, tn=d_mem, name="mem_v")
        x = _cross(x, norm_cross, w_cq, mk, mv, w_co, layer=l, batch=batch, seq=seq, row0=0, tm=tm_cross,
                   heads=mem_heads)
        x = _cross(x, norm_cross, w_cq, cache_mem_k[l].reshape(dec_b * mem_len, d_mem),
                   cache_mem_v[l].reshape(dec_b * mem_len, d_mem), w_co, layer=l,
                   batch=dec_b, seq=dec_seq, row0=n_prompt, tm=dec_seq, heads=mem_heads)

        last = l == depth - 1
        x = _ffn(x, norm_ffb, ffb_w_gate, ffb_w_up, ffb_w_down, norm_final, layer=l, tm=tm, tf=512,
                 final_norm=last, n_sample=n_sample if last else 0)

        outs["kp"].append(proj[:n_prompt, d_sb:2 * d_sb].reshape(batch, seq, heads, HEAD_DIM))
        outs["vp"].append(proj[:n_prompt, 2 * d_sb:3 * d_sb].reshape(batch, seq, heads, HEAD_DIM))
        outs["cp"].append(buf_p[:, HALO - (CONV_W - 1):])
        outs["mk"].append(mk.reshape(batch, mem_len, mem_heads, HEAD_DIM))
        outs["mv"].append(mv.reshape(batch, mem_len, mem_heads, HEAD_DIM))
        outs["ks"].append(proj[n_prompt:, d_sb:2 * d_sb].reshape(dec_b, dec_seq, heads, HEAD_DIM))
        outs["vs"].append(proj[n_prompt:, 2 * d_sb:3 * d_sb].reshape(dec_b, dec_seq, heads, HEAD_DIM))
        outs["cs"].append(buf_s[:, HALO - (CONV_W - 1):])

    y_prompt = x[0].reshape(batch, seq, d)
    y_sample = x[1].reshape(dec_b, dec_seq, d)
    return (y_prompt, y_sample, jnp.stack(outs["kp"]), jnp.stack(outs["vp"]), jnp.stack(outs["cp"]),
            jnp.stack(outs["mk"]), jnp.stack(outs["mv"]), jnp.stack(outs["ks"]), jnp.stack(outs["vs"]),
            jnp.stack(outs["cs"]))
```

```python
import functools
import math

import jax
import jax.numpy as jnp
from jax import lax
from jax.experimental import pallas as pl
from jax.experimental.pallas import tpu as pltpu

EPS = 1e-6
PAGE = 128
HEAD_DIM = 128
SUBLANES = 8
CONV_W = 31
HALO = 32
VMEM_LIMIT = 60 * 1024 * 1024
LOG2E = math.log2(math.e)

BF16 = jnp.bfloat16
F32 = jnp.float32


def _cparams(n_axes):
    return pltpu.CompilerParams(dimension_semantics=("arbitrary",) * n_axes, vmem_limit_bytes=VMEM_LIMIT)


def _rms(x, g):
    return x * lax.rsqrt(jnp.mean(x * x, axis=-1, keepdims=True) + EPS) * g


def _dot(a, b):
    return jnp.dot(a, b, preferred_element_type=F32)


def _dot_nt(a, b):
    return lax.dot_general(a, b, (((1,), (1,)), ((), ())), preferred_element_type=F32)


def _sigmoid(x):
    return 1.0 / (1.0 + jnp.exp(-x))


def _layer_vec(g):
    return g.reshape(g.shape[0], 1, g.shape[1])


def _ffn_kernel(*refs, nf, tf, tail, final_norm, n_tiles, n_main, split_in, split_out):
    refs = list(refs)
    x_ref = refs.pop(0)
    xs_ref = refs.pop(0) if split_in else None
    g_ref, wg_ref, wu_ref, wd_ref, gf_ref, o_ref = refs[:6]
    os_ref = refs[6] if split_out else None
    u_ref = refs[-1]
    i = pl.program_id(0)
    f = pl.program_id(1)

    @pl.when(f == 0)
    def _():
        if split_in:
            @pl.when(i < n_tiles - 1)
            def _():
                o_ref[...] = x_ref[...]

            @pl.when(i == n_tiles - 1)
            def _():
                o_ref[0:n_main, :] = x_ref[0:n_main, :]
                o_ref[n_main:, :] = xs_ref[...]
        else:
            o_ref[...] = x_ref[...]
        u_ref[...] = _rms(o_ref[...], g_ref[...]).astype(BF16)

    def step(width):
        u = u_ref[...]
        gate = _dot(u, wg_ref[:, :width].astype(BF16))
        up = _dot(u, wu_ref[:, :width].astype(BF16))
        h = (0.5 * (gate * _sigmoid(gate) * up)).astype(BF16)
        o_ref[...] += _dot(h, wd_ref[:width, :].astype(BF16))

    if tail == tf:
        step(tf)
    else:
        @pl.when(f < nf - 1)
        def _():
            step(tf)

        @pl.when(f == nf - 1)
        def _():
            step(tail)

    if final_norm or split_out:
        @pl.when(f == nf - 1)
        def _():
            if final_norm:
                o_ref[...] = _rms(o_ref[...], gf_ref[...])
            if split_out:
                @pl.when(i == n_tiles - 1)
                def _():
                    os_ref[...] = o_ref[n_main:, :]


def _ffn(x, g, wg, wu, wd, gfin, *, layer, tm, tf, final_norm, x_sample=None, n_sample=0):
    split_in, split_out = x_sample is not None, n_sample > 0
    d = x.shape[1]
    m = x.shape[0] + (x_sample.shape[0] if split_in else 0)
    n_extra = x_sample.shape[0] if split_in else n_sample
    n_tiles = m // tm
    n_main = tm - n_extra
    assert n_main > 0 and n_main % SUBLANES == 0
    dff = wg.shape[2]
    nf = pl.cdiv(dff, tf)
    tail = dff - (nf - 1) * tf
    kern = functools.partial(_ffn_kernel, nf=nf, tf=tf, tail=tail, final_norm=final_norm, n_tiles=n_tiles,
                             n_main=n_main, split_in=split_in, split_out=split_out)
    rows = lambda: pl.BlockSpec((tm, d), lambda i, f: (i, 0))
    extra = lambda: pl.BlockSpec((n_extra, d), lambda i, f: (0, 0))
    in_specs = [rows()] + ([extra()] if split_in else []) + [
        pl.BlockSpec((None, 1, d), lambda i, f: (layer, 0, 0)),
        pl.BlockSpec((None, d, tf), lambda i, f: (layer, 0, f)),
        pl.BlockSpec((None, d, tf), lambda i, f: (layer, 0, f)),
        pl.BlockSpec((None, tf, d), lambda i, f: (layer, f, 0)),
        pl.BlockSpec((1, d), lambda i, f: (0, 0)),
    ]
    args = [x] + ([x_sample] if split_in else []) + [_layer_vec(g), wg, wu, wd, gfin.reshape(1, d)]
    if split_out:
        out_shape = (jax.ShapeDtypeStruct((m - n_extra, d), F32), jax.ShapeDtypeStruct((n_extra, d), F32))
        out_specs = (rows(), extra())
    else:
        out_shape = jax.ShapeDtypeStruct((m, d), F32)
        out_specs = rows()
    return pl.pallas_call(
        kern,
        out_shape=out_shape,
        grid=(n_tiles, nf),
        in_specs=in_specs,
        out_specs=out_specs,
        scratch_shapes=[pltpu.VMEM((tm, d), BF16)],
        compiler_params=_cparams(2),
        name="ffn",
    )(*args)


def _norm_matmul_kernel(x_ref, g_ref, w_ref, o_ref, u_ref):
    @pl.when(pl.program_id(1) == 0)
    def _():
        u_ref[...] = _rms(x_ref[...], g_ref[...]).astype(BF16)

    o_ref[...] = _dot(u_ref[...], w_ref[...].astype(BF16))


def _norm_matmul(x, g, w, *, layer, tm, tn, name):
    m, d = x.shape
    n = w.shape[2]
    return pl.pallas_call(
        _norm_matmul_kernel,
        out_shape=jax.ShapeDtypeStruct((m, n), F32),
        grid=(m // tm, n // tn),
        in_specs=[
            pl.BlockSpec((tm, d), lambda i, j: (i, 0)),
            pl.BlockSpec((None, 1, d), lambda i, j: (layer, 0, 0)),
            pl.BlockSpec((None, d, tn), lambda i, j: (layer, 0, j)),
        ],
        out_specs=pl.BlockSpec((tm, tn), lambda i, j: (i, j)),
        scratch_shapes=[pltpu.VMEM((tm, d), BF16)],
        compiler_params=_cparams(2),
        name=name,
    )(x, _layer_vec(g), w)


def _in_proj_kernel(x_ref, g_ref, w_ref, o_ref, qkv_ref, u_ref, *, q_blocks, qkv_blocks, q_scale):
    j = pl.program_id(1)

    @pl.when(j == 0)
    def _():
        u_ref[...] = _rms(x_ref[...], g_ref[...]).astype(BF16)

    acc = _dot(u_ref[...], w_ref[...].astype(BF16))
    o_ref[...] = acc

    @pl.when(j < q_blocks)
    def _():
        qkv_ref[...] = (acc * q_scale).astype(BF16)

    @pl.when((j >= q_blocks) & (j < qkv_blocks))
    def _():
        qkv_ref[...] = acc.astype(BF16)


def _in_proj(x, g, w, *, layer, tm, tn, d_sb, q_scale):
    m, d = x.shape
    n = w.shape[2]
    q_blocks, qkv_blocks = d_sb // tn, 3 * d_sb // tn
    kern = functools.partial(_in_proj_kernel, q_blocks=q_blocks, qkv_blocks=qkv_blocks, q_scale=q_scale)
    return pl.pallas_call(
        kern,
        out_shape=(jax.ShapeDtypeStruct((m, n), F32), jax.ShapeDtypeStruct((m, 3 * d_sb), BF16)),
        grid=(m // tm, n // tn),
        in_specs=[
            pl.BlockSpec((tm, d), lambda i, j: (i, 0)),
            pl.BlockSpec((None, 1, d), lambda i, j: (layer, 0, 0)),
            pl.BlockSpec((None, d, tn), lambda i, j: (layer, 0, j)),
        ],
        out_specs=(
            pl.BlockSpec((tm, tn), lambda i, j: (i, j)),
            pl.BlockSpec((tm, tn), lambda i, j: (i, jnp.minimum(j, qkv_blocks - 1))),
        ),
        scratch_shapes=[pltpu.VMEM((tm, d), BF16)],
        compiler_params=_cparams(2),
        name="in_proj",
    )(x, _layer_vec(g), w)


MASKED = -1e30


def _sb_sums(z, tri):
    neg_abs = lax.bitcast_convert_type(lax.bitcast_convert_type(z, jnp.uint32) | jnp.uint32(0x80000000), F32)
    soft = jnp.maximum(z, 0.0) + jnp.log(1.0 + jnp.exp2(neg_abs)) * LOG2E
    parts = []
    for _ in range(tri.shape[0] // z.shape[1]):
        parts.append(soft.astype(BF16))
        soft = soft - parts[-1].astype(F32)
    return _dot(parts[0] if len(parts) == 1 else jnp.concatenate(parts, axis=1), tri)


def _sb_weights(z, csum, carry):
    return jnp.exp2(z - csum - carry), carry + csum[:, 0:1]


def _tri(keys, parts):
    j = lax.broadcasted_iota(jnp.int32, (parts * keys, keys), 0) & (keys - 1)
    s = lax.broadcasted_iota(jnp.int32, (parts * keys, keys), 1)
    return (j >= s).astype(BF16)


def _sbp_kernel(bias_ref, tile_ref, pair_ref, kind_ref, q_ref, k_ref, v_ref, tri_ref, o_ref,
                z_ref, e_ref, tot_ref, mb_ref, *, tq, n_iter):
    bias = bias_ref[pl.program_id(1)]
    tri = tri_ref[...]
    wide = 2 * tq
    diff = lax.broadcasted_iota(jnp.int32, (tq, wide), 1) - lax.broadcasted_iota(jnp.int32, (tq, wide), 0)
    for kind, bound in enumerate((wide, 0, tq)):
        mb_ref[kind] = jnp.where(diff < bound, bias, MASKED)

    def where(t):
        i, p = tile_ref[t], pair_ref[t]
        return i, p, pl.ds(pl.multiple_of(i * tq, tq), tq), pl.ds(pl.multiple_of(p * wide, wide), wide)

    def logits(t):
        _, _, q_rows, k_rows = where(t)
        z_ref[...] = _dot_nt(q_ref[q_rows, :], k_ref[k_rows, :]) + mb_ref[kind_ref[t]]

    def sums():
        for s in range(2):
            z = z_ref[:, s * tq:(s + 1) * tq]
            csum = _sb_sums(z, tri)
            e_ref[:, s * tq:(s + 1) * tq] = z - csum
            tot_ref[s] = csum[:, 0:1]

    def apply(t, carry, acc):
        i, p, q_rows, k_rows = where(t)
        fresh = p == i // 2
        carry = jnp.where(fresh, 0.0, carry)
        acc = jnp.where(fresh, 0.0, acc)
        w1 = jnp.exp2(e_ref[:, tq:] - carry)
        carry = carry + tot_ref[1]
        w0 = jnp.exp2(e_ref[:, :tq] - carry)
        carry = carry + tot_ref[0]
        acc = acc + _dot(jnp.concatenate([w0, w1], axis=1).astype(BF16), v_ref[k_rows, :])
        o_ref[q_rows, :] = acc
        return carry, acc

    state = (jnp.zeros((tq, 1), F32), jnp.zeros((tq, HEAD_DIM), F32))
    logits(0)
    sums()
    logits(1)

    def body(t, st):
        st = apply(t - 2, *st)
        sums()
        logits(t)
        return st

    state = lax.fori_loop(2, n_iter, body, state, unroll=7 if (n_iter - 2) % 7 == 0 else 2)
    state = apply(n_iter - 2, *state)
    sums()
    apply(n_iter - 1, *state)


def _sb_prompt(qkv, bias2, *, n_tok, batch, seq, heads, tq):
    nq = seq // tq
    assert nq % 2 == 0, "key blocks are taken in pairs"
    steps = [(i, p) for i in range(nq) for p in range(i // 2, -1, -1)]
    tiles = jnp.asarray([i for i, _ in steps], jnp.int32)
    pairs = jnp.asarray([p for _, p in steps], jnp.int32)
    kinds = jnp.asarray([0 if p < i // 2 else 1 + i % 2 for i, p in steps], jnp.int32)
    kern = functools.partial(_sbp_kernel, tq=tq, n_iter=len(steps))
    grid_spec = pltpu.PrefetchScalarGridSpec(
        num_scalar_prefetch=4,
        grid=(batch, heads),
        in_specs=[
            pl.BlockSpec((seq, HEAD_DIM), lambda b, h, *_: (b, h)),
            pl.BlockSpec((seq, HEAD_DIM), lambda b, h, *_: (b, heads + h)),
            pl.BlockSpec((seq, HEAD_DIM), lambda b, h, *_: (b, 2 * heads + h)),
            pl.BlockSpec((tq, tq), lambda b, h, *_: (0, 0)),
        ],
        out_specs=pl.BlockSpec((seq, HEAD_DIM), lambda b, h, *_: (b, h)),
        scratch_shapes=[
            pltpu.VMEM((tq, 2 * tq), F32),
            pltpu.VMEM((tq, 2 * tq), F32),
            pltpu.VMEM((2, tq, 1), F32),
            pltpu.VMEM((3, tq, 2 * tq), F32),
        ],
    )
    return pl.pallas_call(
        kern,
        out_shape=jax.ShapeDtypeStruct((n_tok, heads * HEAD_DIM), F32),
        grid_spec=grid_spec,
        compiler_params=_cparams(2),
        name="sb_prompt",
    )(bias2, tiles, pairs, kinds, qkv, qkv, qkv, _tri(tq, 1))


def _sbs_kernel(pt_ref, q_ref, kn_ref, vn_ref, *rest, heads, lq, pages, n_steps, q_scale):
    kp_refs = rest[:pages]
    vp_refs = rest[pages:2 * pages]
    bias_ref, tri_ref, _, o_ref, carry_ref, acc_ref = rest[2 * pages:]
    j = pl.program_id(1)
    pad = jnp.zeros((lq, HEAD_DIM), F32)

    def q_head(h):
        qh = q_ref[:, h * HEAD_DIM:(h + 1) * HEAD_DIM] * q_scale
        return jnp.concatenate([qh, pad], axis=0).astype(BF16)

    def attend(k_heads, v_heads, mask):
        rows = heads * lq
        z = jnp.concatenate(
            [jnp.concatenate([_dot_nt(q_head(h), k_head(h))[0:lq] for h in range(heads)], axis=0) + bias_ref[...]
             for k_head in k_heads], axis=0)
        if mask is not None:
            z = jnp.where(mask, z, MASKED)
        csum = _sb_sums(z, tri_ref[...])
        carry = carry_ref[...]
        acc = [acc_ref[h * lq:(h + 1) * lq, :] for h in range(heads)]
        for s, v_head in enumerate(v_heads):
            w, carry = _sb_weights(z[s * rows:(s + 1) * rows], csum[s * rows:(s + 1) * rows], carry)
            for h in range(heads):
                wh = jnp.concatenate([w[h * lq:(h + 1) * lq], pad], axis=0).astype(BF16)
                acc[h] = acc[h] + _dot(wh, v_head(h))[0:lq]
        carry_ref[...] = carry
        acc_ref[...] = jnp.concatenate(acc, axis=0)

    @pl.when(j == 0)
    def _():
        carry_ref[...] = jnp.zeros_like(carry_ref)
        acc_ref[...] = jnp.zeros_like(acc_ref)
        fill = jnp.zeros((PAGE - lq, HEAD_DIM), F32)
        new = lambda ref: lambda h: jnp.concatenate(
            [ref[:, h * HEAD_DIM:(h + 1) * HEAD_DIM], fill], axis=0).astype(BF16)
        key = lax.broadcasted_iota(jnp.int32, (heads * lq, PAGE), 1)
        t = lax.broadcasted_iota(jnp.int32, (heads * lq, PAGE), 0) & (lq - 1)
        attend([new(kn_ref)], [new(vn_ref)], key < t)

    @pl.when(j > 0)
    def _():
        cached = lambda ref: lambda h: ref[pl.ds(h, PAGE, stride=heads), :].astype(BF16)
        attend([cached(r) for r in kp_refs], [cached(r) for r in vp_refs], None)

    @pl.when(j == n_steps - 1)
    def _():
        for h in range(heads):
            o_ref[:, h * HEAD_DIM:(h + 1) * HEAD_DIM] = acc_ref[h * lq:(h + 1) * lq, :]


def _sb_sample(attn, proj, cache_k, cache_v, page_table, bias2, *, layer, row0, heads, lq, q_scale):
    dec_b, n_pages = page_table.shape
    assert lq & (lq - 1) == 0
    width = heads * HEAD_DIM
    rows = heads * lq
    pages = max(p for p in (16, 8, 4, 2, 1) if n_pages % p == 0)
    n_steps = n_pages // pages + 1
    rb0 = row0 // lq
    kern = functools.partial(_sbs_kernel, heads=heads, lq=lq, pages=pages, n_steps=n_steps, q_scale=q_scale)

    def page(s):
        def index(b, j, pt):
            pos = n_pages - 1 - (jnp.maximum(j, 1) - 1) * pages - s
            return (layer, pt[b, pos], 0, 0)
        return pl.BlockSpec((None, None, PAGE * heads, HEAD_DIM), index)

    row_spec = lambda col: pl.BlockSpec((lq, width), lambda b, j, pt: (rb0 + b, col))
    grid_spec = pltpu.PrefetchScalarGridSpec(
        num_scalar_prefetch=1,
        grid=(dec_b, n_steps),
        in_specs=[row_spec(0), row_spec(1), row_spec(2)]
        + [page(s) for s in range(pages)] + [page(s) for s in range(pages)]
        + [
            pl.BlockSpec((rows, PAGE), lambda b, j, pt: (0, 0)),
            pl.BlockSpec((2 * PAGE, PAGE), lambda b, j, pt: (0, 0)),
            pl.BlockSpec(memory_space=pl.ANY),
        ],
        out_specs=row_spec(0),
        scratch_shapes=[pltpu.VMEM((rows, 1), F32), pltpu.VMEM((rows, HEAD_DIM), F32)],
    )
    bias_rows = jnp.broadcast_to(jnp.repeat(bias2, lq)[:, None], (rows, PAGE))
    n_in = 1 + 3 + 2 * pages + 3
    return pl.pallas_call(
        kern,
        out_shape=jax.ShapeDtypeStruct(attn.shape, attn.dtype),
        grid_spec=grid_spec,
        input_output_aliases={n_in - 1: 0},
        compiler_params=_cparams(2),
        name="sb_sample",
    )(page_table, proj, proj, proj, *([cache_k] * pages), *([cache_v] * pages), bias_rows, _tri(PAGE, 2), attn)


def _conv_kernel(cv_ref, cg_ref, init_ref, w_ref, b_ref, lg_ref, lb_ref, *rest, tq, nt, rc, cc):
    y_ref, buf_ref, ext_ref, rot_ref, acc_ref = rest[-5:]
    i = pl.program_id(1)
    ch = cv_ref.shape[1]

    @pl.when(i == 0)
    def _():
        ext_ref[0:HALO, :] = init_ref[...]
        ext_ref[HALO + tq:HALO + tq + SUBLANES, :] = jnp.zeros((SUBLANES, ch), F32)

    ext_ref[HALO:HALO + tq, :] = cv_ref[...] * _sigmoid(cg_ref[...])
    first = HALO - (CONV_W - 1)
    span = rot_ref.shape[1]
    for s in range(SUBLANES):
        rot_ref[s] = ext_ref[first + s:first + s + span, :]
    for r in range(0, tq, rc):
        for c in range(0, ch, cc):
            acc = jnp.broadcast_to(b_ref[:, c:c + cc], (rc, cc))
            for j in range(CONV_W):
                a, s = divmod(j, SUBLANES)
                row = r + SUBLANES * a
                acc = acc + rot_ref[s, row:row + rc, c:c + cc] * w_ref[j:j + 1, c:c + cc]
            acc_ref[r:r + rc, c:c + cc] = acc
    y = acc_ref[...]
    mu = jnp.mean(y, axis=-1, keepdims=True)
    yc = y - mu
    y = yc * lax.rsqrt(jnp.mean(yc * yc, axis=-1, keepdims=True) + EPS) * lg_ref[...] + lb_ref[...]
    y_ref[...] = (y * _sigmoid(y)).astype(y_ref.dtype)

    @pl.when(i == nt - 1)
    def _():
        buf_ref[...] = ext_ref[tq:tq + HALO, :]

    if nt > 1:
        @pl.when(i < nt - 1)
        def _():
            ext_ref[0:HALO, :] = ext_ref[tq:tq + HALO, :]


def _conv(yact, proj, init, w_dw, b_dw, ln_g, ln_b, *, layer, batch, seq, row0, tq, val_blk):
    ch = w_dw.shape[2]
    aliased = yact is not None
    nt = seq // tq
    rb0 = row0 // tq
    rc = min(tq, 64)
    kern = functools.partial(_conv_kernel, tq=tq, nt=nt, rc=rc, cc=256)
    row_map = lambda b, i: (rb0 + b * nt + i, 0)
    vec = pl.BlockSpec((None, 1, ch), lambda b, i: (layer, 0, 0))
    in_specs = [
        pl.BlockSpec((tq, ch), lambda b, i: (rb0 + b * nt + i, val_blk)),
        pl.BlockSpec((tq, ch), lambda b, i: (rb0 + b * nt + i, val_blk + 1)),
        pl.BlockSpec((None, HALO, ch), lambda b, i: (b, 0, 0)),
        pl.BlockSpec((None, CONV_W, ch), lambda b, i: (layer, 0, 0)),
        vec, vec, vec,
    ]
    args = [proj, proj, init, w_dw, _layer_vec(b_dw), _layer_vec(ln_g), _layer_vec(ln_b)]
    if aliased:
        in_specs.append(pl.BlockSpec(memory_space=pl.ANY))
        args.append(yact)
    return pl.pallas_call(
        kern,
        out_shape=(jax.ShapeDtypeStruct((proj.shape[0], ch), F32), jax.ShapeDtypeStruct((batch, HALO, ch), F32)),
        grid=(batch, nt),
        in_specs=in_specs,
        out_specs=(pl.BlockSpec((tq, ch), row_map), pl.BlockSpec((None, HALO, ch), lambda b, i: (b, 0, 0))),
        scratch_shapes=[
            pltpu.VMEM((HALO + tq + SUBLANES, ch), F32),
            pltpu.VMEM((SUBLANES, tq + SUBLANES * ((CONV_W - 1) // SUBLANES), ch), F32),
            pltpu.VMEM((tq, ch), F32),
        ],
        input_output_aliases={len(args) - 1: 0} if aliased else {},
        compiler_params=_cparams(2),
        name="conv_sample" if aliased else "conv_prompt",
    )(*args)


def _mix_kernel(x_ref, a_ref, y_ref, gs_ref, gc_ref, wsb_ref, wcv_ref, wmx_ref, o_ref, *, nn):
    n = pl.program_id(1)

    @pl.when(n == 0)
    def _():
        o_ref[...] = jnp.zeros_like(o_ref)

    o_sb = _dot(a_ref[...].astype(BF16), wsb_ref[...].astype(BF16))
    o_cv = _dot(y_ref[...].astype(BF16), wcv_ref[...].astype(BF16))
    mixed = _sigmoid(gs_ref[...]) * o_sb + _sigmoid(gc_ref[...]) * o_cv
    o_ref[...] += _dot(mixed.astype(BF16), wmx_ref[...].astype(BF16))

    @pl.when(n == nn - 1)
    def _():
        o_ref[...] = x_ref[...] + o_ref[...]


def _mix(x, attn, yact, proj, w_sb_o, w_conv_o, w_mix_o, *, layer, tm, tn, gsb_col, gconv_col):
    m, d = x.shape
    d_sb = attn.shape[1]
    d_conv = yact.shape[1]
    nn = d // tn
    kern = functools.partial(_mix_kernel, nn=nn)
    return pl.pallas_call(
        kern,
        out_shape=jax.ShapeDtypeStruct((m, d), F32),
        grid=(m // tm, nn),
        in_specs=[
            pl.BlockSpec((tm, d), lambda i, n: (i, 0)),
            pl.BlockSpec((tm, d_sb), lambda i, n: (i, 0)),
            pl.BlockSpec((tm, d_conv), lambda i, n: (i, 0)),
            pl.BlockSpec((tm, tn), lambda i, n: (i, gsb_col // tn + n)),
            pl.BlockSpec((tm, tn), lambda i, n: (i, gconv_col // tn + n)),
            pl.BlockSpec((None, d_sb, tn), lambda i, n: (layer, 0, n)),
            pl.BlockSpec((None, d_conv, tn), lambda i, n: (layer, 0, n)),
            pl.BlockSpec((None, tn, d), lambda i, n: (layer, n, 0)),
        ],
        out_specs=pl.BlockSpec((tm, d), lambda i, n: (i, 0)),
        compiler_params=_cparams(2),
        name="mix",
    )(x, attn, yact, proj, proj, w_sb_o, w_conv_o, w_mix_o)


def _cross_kernel(x_ref, g_ref, wq_ref, mk_ref, mv_ref, wo_ref, o_ref, *, heads, scale):
    x = x_ref[...]
    u = _rms(x, g_ref[...]).astype(BF16)
    q = _dot(u, wq_ref[...].astype(BF16))
    outs = []
    for h in range(heads):
        sl = slice(h * HEAD_DIM, (h + 1) * HEAD_DIM)
        s = _dot_nt(q[:, sl].astype(BF16), mk_ref[:, sl].astype(BF16)) * scale
        p = jnp.exp(s - jnp.max(s, axis=-1, keepdims=True))
        o = _dot(p.astype(BF16), mv_ref[:, sl].astype(BF16)) / jnp.sum(p, axis=-1, keepdims=True)
        outs.append(o.astype(BF16))
    o_ref[...] = x + _dot(jnp.concatenate(outs, axis=1), wo_ref[...].astype(BF16))


def _cross(x, g, w_cq, mem_k, mem_v, w_co, *, layer, batch, seq, row0, tm, heads):
    m, d = x.shape
    d_mem = w_cq.shape[2]
    mem_len = mem_k.shape[0] // batch
    nt = seq // tm
    rb0 = row0 // tm
    kern = functools.partial(_cross_kernel, heads=heads, scale=HEAD_DIM ** -0.5)
    return pl.pallas_call(
        kern,
        out_shape=jax.ShapeDtypeStruct((m, d), F32),
        grid=(batch, nt),
        in_specs=[
            pl.BlockSpec((tm, d), lambda b, i: (rb0 + b * nt + i, 0)),
            pl.BlockSpec((None, 1, d), lambda b, i: (layer, 0, 0)),
            pl.BlockSpec((None, d, d_mem), lambda b, i: (layer, 0, 0)),
            pl.BlockSpec((mem_len, d_mem), lambda b, i: (b, 0)),
            pl.BlockSpec((mem_len, d_mem), lambda b, i: (b, 0)),
            pl.BlockSpec((None, d_mem, d), lambda b, i: (layer, 0, 0)),
        ],
        out_specs=pl.BlockSpec((tm, d), lambda b, i: (rb0 + b * nt + i, 0)),
        input_output_aliases={0: 0},
        compiler_params=_cparams(2),
        name="cross",
    )(x, _layer_vec(g), w_cq, mem_k, mem_v, w_co)


def _largest_tile(n, cap, mult):
    best = None
    for t in range(mult, cap + 1, mult):
        if n % t == 0:
            best = t
    assert best is not None, (n, cap, mult)
    return best


def kernel(x_prompt, x_sample, mem_prompt, cache_k, cache_v, state_conv, cache_mem_k, cache_mem_v, page_table, norm_ffa, ffa_w_gate, ffa_w_up, ffa_w_down, norm_mix, w_in, sb_bias, conv_w_dw, conv_b_dw, conv_ln_g, conv_ln_b, w_conv_o, w_sb_o, w_mix_o, norm_cross, norm_mem, w_cq, w_ck, w_cv, w_co, norm_ffb, ffb_w_gate, ffb_w_up, ffb_w_down, norm_final):
    batch, seq, d = x_prompt.shape
    dec_b, dec_seq, _ = x_sample.shape
    depth = w_in.shape[0]
    heads = cache_k.shape[3]
    d_sb = heads * HEAD_DIM
    d_conv = conv_w_dw.shape[2]
    mem_len = mem_prompt.shape[1]
    mem_heads = cache_mem_k.shape[3]
    d_mem = mem_heads * HEAD_DIM
    n_prompt = batch * seq
    n_tok = n_prompt + dec_b * dec_seq
    val_col, gsb_col, gconv_col = 3 * d_sb, 3 * d_sb + 2 * d_conv, 3 * d_sb + 2 * d_conv + d
    assert d_conv == d_sb and val_col % d_conv == 0

    tm = _largest_tile(n_tok, 768, 16)
    tm_big = _largest_tile(n_tok, 1536, 16)
    tn_in =_largest_tile(math.gcd(w_in.shape[2], d_sb), 512, 128)
    tq_attn = min(seq, 256)
    tq_conv = min(seq, 256)
    tm_cross = min(seq, 512)
    q_scale = HEAD_DIM ** -0.5 * LOG2E
    bias2 = sb_bias * LOG2E

    n_sample = dec_b * dec_seq
    x = None
    mem = mem_prompt.reshape(batch * mem_len, d)
    w_sb_o, w_conv_o, w_mix_o = w_sb_o.astype(BF16), w_conv_o.astype(BF16), w_mix_o.astype(BF16)
    ck = cache_k.reshape(depth, cache_k.shape[1], PAGE * heads, HEAD_DIM)
    cv = cache_v.reshape(depth, cache_v.shape[1], PAGE * heads, HEAD_DIM)
    zero_buf = jnp.zeros((batch, HALO, d_conv), F32)
    state_pad = jnp.pad(state_conv, ((0, 0), (0, 0), (HALO - (CONV_W - 1), 0), (0, 0)))

    outs = {k: [] for k in ("kp", "vp", "cp", "mk", "mv", "ks", "vs", "cs")}
    for l in range(depth):
        if l == 0:
            x = _ffn(x_prompt.reshape(n_prompt, d), norm_ffa, ffa_w_gate, ffa_w_up, ffa_w_down, norm_final, layer=l,
                     tm=tm, tf=512, final_norm=False, x_sample=x_sample.reshape(n_sample, d))
        else:
            x = _ffn(x, norm_ffa, ffa_w_gate, ffa_w_up, ffa_w_down, norm_final, layer=l, tm=tm, tf=512,
                     final_norm=False)
        proj, qkv = _in_proj(x, norm_mix, w_in, layer=l, tm=tm_big, tn=tn_in, d_sb=d_sb, q_scale=q_scale)

        attn = _sb_prompt(qkv, bias2[l], n_tok=n_tok, batch=batch, seq=seq, heads=heads, tq=tq_attn)
        attn = _sb_sample(attn, proj, ck, cv, page_table, bias2[l], layer=l, row0=n_prompt, heads=heads, lq=dec_seq,
                          q_scale=q_scale)

        yact, buf_p = _conv(None, proj, zero_buf, conv_w_dw, conv_b_dw, conv_ln_g, conv_ln_b, layer=l,
                            batch=batch, seq=seq, row0=0, tq=tq_conv, val_blk=val_col // d_conv)
        yact, buf_s = _conv(yact, proj, state_pad[l], conv_w_dw, conv_b_dw, conv_ln_g, conv_ln_b, layer=l,
                            batch=dec_b, seq=dec_seq, row0=n_prompt, tq=dec_seq, val_blk=val_col // d_conv)

        x = _mix(x, attn, yact, proj, w_sb_o, w_conv_o, w_mix_o, layer=l, tm=tm, tn=512 if d % 512 == 0 else 256,
                 gsb_col=gsb_col, gconv_col=gconv_col)

        mk = _norm_matmul(mem, norm_mem, w_ck, layer=l, tm=batch * mem_len, tn=d_mem, name="mem_k")
        mv = _norm_matmul(mem, norm_mem, w_cv, layer=l, tm=batch * mem_len, tn=d_mem, name="mem_v")
        x = _cross(x, norm_cross, w_cq, mk, mv, w_co, layer=l, batch=batch, seq=seq, row0=0, tm=tm_cross,
                   heads=mem_heads)
        x = _cross(x, norm_cross, w_cq, cache_mem_k[l].reshape(dec_b * mem_len, d_mem),
                   cache_mem_v[l].reshape(dec_b * mem_len, d_mem), w_co, layer=l,
                   batch=dec_b, seq=dec_seq, row0=n_prompt, tm=dec_seq, heads=mem_heads)

        last = l == depth - 1
        x = _ffn(x, norm_ffb, ffb_w_gate, ffb_w_up, ffb_w_down, norm_final, layer=l, tm=tm, tf=512,
                 final_norm=last, n_sample=n_sample if last else 0)

        outs["kp"].append(proj[:n_prompt, d_sb:2 * d_sb].reshape(batch, seq, heads, HEAD_DIM))
        outs["vp"].append(proj[:n_prompt, 2 * d_sb:3 * d_sb].reshape(batch, seq, heads, HEAD_DIM))
        outs["cp"].append(buf_p[:, HALO - (CONV_W - 1):])
        outs["mk"].append(mk.reshape(batch, mem_len, mem_heads, HEAD_DIM))
        outs["mv"].append(mv.reshape(batch, mem_len, mem_heads, HEAD_DIM))
        outs["ks"].append(proj[n_prompt:, d_sb:2 * d_sb].reshape(dec_b, dec_seq, heads, HEAD_DIM))
        outs["vs"].append(proj[n_prompt:, 2 * d_sb:3 * d_sb].reshape(dec_b, dec_seq, heads, HEAD_DIM))
        outs["cs"].append(buf_s[:, HALO - (CONV_W - 1):])

    y_prompt = x[0].reshape(batch, seq, d)
    y_sample = x[1].reshape(dec_b, dec_seq, d)
    return (y_prompt, y_sample, jnp.stack(outs["kp"]), jnp.stack(outs["vp"]), jnp.stack(outs["cp"]),
            jnp.stack(outs["mk"]), jnp.stack(outs["mv"]), jnp.stack(outs["ks"]), jnp.stack(outs["vs"]),
            jnp.stack(outs["cs"]))
```
